```python
import jax, jax.numpy as jnp
from jax import lax
import numpy as np

D_MODEL = 4096
BATCH = 2
SEQ = 8192
DEPTH = 2

F32 = jnp.float32
GRID_W = 64
MIX_W = D_MODEL
RET_HEADS = 8
RET_DK = MIX_W // 2 // RET_HEADS
RET_W = RET_HEADS * RET_DK
RET_CHUNK = 128
RET_DECAY_BASE = 5.0
ROPE_BASE = 10000.0
NA_HEADS = 16
NA_DH = (MIX_W - RET_W) // NA_HEADS
NA_W = NA_HEADS * NA_DH
NA_KH = 8
NA_KW = 16
NA_QBLK = 16
NA_SPAN = 2 * NA_KW
HG_EXPAND = 128
HG_HEADS = MIX_W // HG_EXPAND
HG_DK = HG_EXPAND
HG_DV = MIX_W // HG_HEADS
HG_F = HG_HEADS * HG_DK
HG_V = HG_HEADS * HG_DV
HG_CHUNK = 64
N_EXPERTS = 64
TOP_K = 8
N_GROUPS = 8
TOPK_GROUPS = 4
E_PER_GROUP = N_EXPERTS // N_GROUPS
D_EXPERT = 384
D_SHARED = 384
ROUTED_SCALE = 2.5
MOE_BLOCK = 256
DEEPNORM_ALPHA = (2.0 * DEPTH) ** 0.25
DEEPNORM_BETA = (8.0 * DEPTH) ** -0.25
LN_EPS = 1e-5
NORM_EPS = 1e-6
N_EVEN = (DEPTH + 1) // 2
N_ODD = DEPTH // 2
EV_SPLITS = (RET_W, RET_W, RET_W, RET_W, NA_W, NA_W, NA_W)
EV_SCALES = (1.0, 1.0, DEEPNORM_BETA, 1.0, 1.0, 1.0, DEEPNORM_BETA)
EV_PROJ = sum(EV_SPLITS)
OD_SPLITS = (HG_F, HG_F, HG_F, HG_V, HG_V)
OD_SCALES = (1.0, 1.0, 1.0, DEEPNORM_BETA, 1.0)
OD_PROJ = sum(OD_SPLITS)

kernel_name = 'hybrid_retention_natten_hgrn2_moe_encoder'


def layer_norm(x, g, b):
    xf = x.astype(F32)
    mu = jnp.mean(xf, -1, keepdims=True)
    var = jnp.mean(jnp.square(xf - mu), -1, keepdims=True)
    return ((xf - mu) * lax.rsqrt(var + LN_EPS) * g.astype(F32) + b.astype(F32)).astype(x.dtype)


def flip_t(a):
    return jnp.flip(a, axis=2)


def rotary(t):
    T, d = t.shape[2], t.shape[3]
    inv = ROPE_BASE ** (-jnp.linspace(0.0, 1.0, d // 2, dtype=F32))
    ang = jnp.arange(T, dtype=F32)[:, None] * inv[None, :]
    cos, sin = jnp.cos(ang), jnp.sin(ang)
    t1, t2 = t[..., : d // 2], t[..., d // 2:]
    return jnp.concatenate([t1 * cos - t2 * sin, t1 * sin + t2 * cos], -1)


def retention_dir(q, k, v, log_g, include_diag):
    B, H, T, dk = q.shape
    dv = v.shape[-1]
    C = RET_CHUNK
    N = T // C
    qc = q.reshape(B, H, N, C, dk)
    kc = k.reshape(B, H, N, C, dk)
    vc = v.reshape(B, H, N, C, dv)
    pos = jnp.arange(C, dtype=F32)
    dist = pos[:, None] - pos[None, :]
    allowed = (dist >= 0) if include_diag else (dist > 0)
    decay = jnp.where(allowed[None], jnp.exp(log_g[:, None, None] * jnp.maximum(dist, 0.0)[None]), 0.0)
    scores = jnp.einsum('bhnid,bhnjd->bhnij', qc, kc) * decay[:, None]
    o = jnp.einsum('bhnij,bhnjv->bhniv', scores, vc)
    lg = log_g[:, None, None, None]
    k_dec = kc * jnp.exp(lg * (C - 1 - pos)[None, None, :, None])
    kv = jnp.einsum('bhncd,bhncv->bhndv', k_dec, vc)
    chunk_decay = jnp.exp(log_g * C)[None, :, None, None]

    def step(S, kv_n):
        return chunk_decay * S + kv_n, S

    _, s_prev = lax.scan(step, jnp.zeros((B, H, dk, dv), F32), jnp.moveaxis(kv, 2, 0))
    s_prev = jnp.moveaxis(s_prev, 0, 2)
    q_dec = qc * jnp.exp(lg * (pos + 1.0)[None, None, :, None])
    o = o + jnp.einsum('bhncd,bhndv->bhncv', q_dec, s_prev)
    return o.reshape(B, H, T, dv)


def neighborhood_attention(q, k, v, rpb):
    B, T, Hh, d = q.shape
    rows = T // GRID_W
    kh = min(NA_KH, rows)
    qg = q.reshape(B, rows, GRID_W, Hh, d)
    kg = k.reshape(B, rows, GRID_W, Hh, d)
    vg = v.reshape(B, rows, GRID_W, Hh, d)
    ncb = GRID_W // NA_QBLK
    qcol = np.arange(GRID_W).reshape(ncb, NA_QBLK)
    kstart = np.clip(np.arange(ncb) * NA_QBLK - NA_KW // 2, 0, GRID_W - NA_SPAN)
    kcol = kstart[:, None] + np.arange(NA_SPAN)[None, :]
    wstart = np.clip(qcol - NA_KW // 2, 0, GRID_W - NA_KW)
    kc3 = kcol[:, None, :]
    valid = (kc3 >= wstart[..., None]) & (kc3 < wstart[..., None] + NA_KW)
    dc = np.clip(kc3 - qcol[..., None], -(NA_KW - 1), NA_KW - 1) + NA_KW - 1
    bias_c = rpb[:, :, dc]
    valid_b = jnp.asarray(valid)[:, :, None, :]
    scale = d ** -0.5

    def one_row(r):
        r0 = jnp.clip(r - kh // 2, 0, rows - kh)
        q_r = lax.dynamic_index_in_dim(qg, r, axis=1, keepdims=False)
        k_r = lax.dynamic_slice_in_dim(kg, r0, kh, axis=1)
        v_r = lax.dynamic_slice_in_dim(vg, r0, kh, axis=1)
        q_b = q_r.reshape(B, ncb, NA_QBLK, Hh, d)
        k_b = k_r[:, :, kcol]
        v_b = v_r[:, :, kcol]
        s = jnp.einsum('bnqhd,binchd->bhnqic', q_b, k_b) * scale
        dr = r0 + jnp.arange(kh) - r + NA_KH - 1
        bias = jnp.take(bias_c, dr, axis=1).transpose(0, 2, 3, 1, 4)
        s = jnp.where(valid_b, s + bias[None], -jnp.inf)
        p = jax.nn.softmax(s.reshape(B, Hh, ncb, NA_QBLK, kh * NA_SPAN), axis=-1)
        p = p.reshape(B, Hh, ncb, NA_QBLK, kh, NA_SPAN)
        o = jnp.einsum('bhnqic,binchd->bnqhd', p, v_b)
        return o.reshape(B, GRID_W, Hh, d)

    out = lax.map(one_row, jnp.arange(rows))
    return out.transpose(1, 0, 2, 3, 4).reshape(B, T, Hh * d)


def even_mixer(x, w_in, decay_exp, gn_w, rpb, w_out):
    B, T, _ = x.shape
    proj = x @ w_in
    rq, rk, rv, rg, nq, nk, nv = jnp.split(proj, np.cumsum(EV_SPLITS)[:-1].tolist(), axis=-1)
    heads = lambda a: a.astype(F32).reshape(B, T, RET_HEADS, RET_DK).transpose(0, 2, 1, 3)
    q = rotary(heads(rq))
    k = rotary(heads(rk)) * (RET_DK ** -0.5)
    v = heads(rv)
    log_g = jnp.log1p(-jnp.exp2(-decay_exp.astype(F32)))
    o = retention_dir(q, k, v, log_g[0], True) + flip_t(
        retention_dir(flip_t(q), flip_t(k), flip_t(v), log_g[1], False))
    mu = jnp.mean(o, -1, keepdims=True)
    var = jnp.mean(jnp.square(o - mu), -1, keepdims=True)
    o = (o - mu) * lax.rsqrt(var + NORM_EPS)
    o_ret = o.transpose(0, 2, 1, 3).reshape(B, T, RET_W) * gn_w.astype(F32) * jax.nn.silu(rg.astype(F32))
    na = lambda a: a.astype(F32).reshape(B, T, NA_HEADS, NA_DH)
    o_na = neighborhood_attention(na(nq), na(nk), na(nv), rpb.astype(F32))
    mixed = jnp.concatenate([o_ret, o_na], axis=-1).astype(x.dtype)
    return mixed @ w_out


def hgrn2_dir(q, k, v, log_f):
    B, H, T, dk = q.shape
    dv = v.shape[-1]
    C = HG_CHUNK
    N = T // C
    chunks = lambda a: jnp.moveaxis(a.reshape(B, H, N, C, a.shape[-1]), 2, 0)
    tri = jnp.tril(jnp.ones((C, C), dtype=bool))[:, :, None]

    def step(S, inp):
        qc, kc, vc, lf = inp
        b = jnp.cumsum(lf, axis=2)
        rel = jnp.where(tri, b[:, :, :, None, :] - b[:, :, None, :, :], -jnp.inf)
        att = jnp.einsum('bhtd,bhsd,bhtsd->bhts', qc, kc, jnp.exp(rel))
        o = jnp.einsum('bhts,bhsv->bhtv', att, vc) + jnp.einsum('bhtd,bhdv->bhtv', qc * jnp.exp(b), S)
        b_end = b[:, :, -1:, :]
        S = jnp.exp(b_end)[:, :, 0, :, None] * S + jnp.einsum('bhsd,bhsv->bhdv', kc * jnp.exp(b_end - b), vc)
        return S, o

    _, o = lax.scan(step, jnp.zeros((B, H, dk, dv), F32),
                    (chunks(q), chunks(k), chunks(v), chunks(log_f)))
    return jnp.moveaxis(o, 0, 2).reshape(B, H, T, dv)


def odd_mixer(x, w_in, lb, norm_w, w_out):
    B, T, _ = x.shape
    proj = x @ w_in
    zq, zf, zb, zi, zg = jnp.split(proj, np.cumsum(OD_SPLITS)[:-1].tolist(), axis=-1)
    heads = lambda a, dh: a.astype(F32).reshape(B, T, HG_HEADS, dh).transpose(0, 2, 1, 3)
    q = jax.nn.silu(heads(zq, HG_DK))
    i = heads(zi, HG_DV)
    lb = lb.astype(F32).reshape(HG_HEADS, 1, HG_DK)
    log_lb, log_1mlb = jnp.log(lb), jnp.log1p(-lb)

    def forget(z):
        z = heads(z, HG_DK)
        log_f = jnp.logaddexp(log_lb, log_1mlb + jax.nn.log_sigmoid(z))
        k = (1.0 - lb) * jax.nn.sigmoid(-z)
        return k, log_f

    kf, lff = forget(zf)
    kb, lfb = forget(zb)
    o = hgrn2_dir(q, kf, i, lff) + flip_t(hgrn2_dir(flip_t(q), flip_t(kb), flip_t(i), flip_t(lfb)))
    o = o * lax.rsqrt(jnp.mean(jnp.square(o), -1, keepdims=True) + NORM_EPS)
    o = o.transpose(0, 2, 1, 3).reshape(B, T, HG_V) * norm_w.astype(F32) * jax.nn.silu(zg.astype(F32))
    return o.astype(x.dtype) @ w_out


def moe_ffn(h, w_router, r_bias, w_gate, w_up, w_down, s_gate, s_up, s_down):
    B, T, D = h.shape
    n_tok = B * T
    xt = h.reshape(n_tok, D)
    scores = jax.nn.sigmoid((xt @ w_router).astype(F32))
    sel = scores + r_bias.astype(F32)
    grp_score = lax.top_k(sel.reshape(n_tok, N_GROUPS, E_PER_GROUP), 2)[0].sum(-1)
    _, top_grp = lax.top_k(grp_score, TOPK_GROUPS)
    grp_ok = jnp.any(top_grp[:, :, None] == jnp.arange(N_GROUPS)[None, None, :], axis=1)
    sel = jnp.where(jnp.repeat(grp_ok, E_PER_GROUP, axis=1), sel, -jnp.inf)
    _, idx = lax.top_k(sel, TOP_K)
    gw = jnp.take_along_axis(scores, idx, axis=1)
    gw = (gw / jnp.sum(gw, -1, keepdims=True) * ROUTED_SCALE).astype(xt.dtype)
    n_assign = n_tok * TOP_K
    e_flat = idx.reshape(n_assign)
    t_flat = jnp.repeat(jnp.arange(n_tok, dtype=jnp.int32), TOP_K)
    g_flat = gw.reshape(n_assign)
    order = jnp.argsort(e_flat)
    e_s, t_s, g_s = e_flat[order], t_flat[order], g_flat[order]
    counts = jnp.bincount(e_flat, length=N_EXPERTS)
    starts = jnp.cumsum(counts) - counts
    pcounts = (counts + MOE_BLOCK - 1) // MOE_BLOCK * MOE_BLOCK
    pends = jnp.cumsum(pcounts)
    pstarts = pends - pcounts
    dest = pstarts[e_s] + jnp.arange(n_assign, dtype=jnp.int32) - starts[e_s]
    n_blocks = -(-n_assign // MOE_BLOCK) + N_EXPERTS
    n_rows = n_blocks * MOE_BLOCK
    tok_buf = jnp.zeros((n_rows,), jnp.int32).at[dest].set(t_s)
    gate_buf = jnp.zeros((n_rows,), xt.dtype).at[dest].set(g_s)
    blk_e = jnp.minimum(jnp.searchsorted(pends, jnp.arange(n_blocks, dtype=jnp.int32) * MOE_BLOCK, side='right'),
                        N_EXPERTS - 1)

    def body(out, blk):
        tok, g, e = blk
        xb = xt[tok]
        y = (jax.nn.silu(xb @ w_gate[e]) * (xb @ w_up[e])) @ w_down[e]
        return out.at[tok].add(y * g[:, None]), None

    routed, _ = lax.scan(body, jnp.zeros_like(xt),
                         (tok_buf.reshape(n_blocks, MOE_BLOCK), gate_buf.reshape(n_blocks, MOE_BLOCK), blk_e))
    shared = (jax.nn.silu(xt @ s_gate) * (xt @ s_up)) @ s_down
    return (routed + shared).reshape(B, T, D)


def setup_inputs(seed: int = 0) -> dict:
    key = jax.random.key(seed)
    ks = jax.random.split(key, 20)
    nrm = lambda k, shape, s: jax.random.normal(k, shape, F32) * s
    sd = D_MODEL ** -0.5
    ev_cols = jnp.concatenate([jnp.full((n,), s, F32) for n, s in zip(EV_SPLITS, EV_SCALES)])
    od_cols = jnp.concatenate([jnp.full((n,), s, F32) for n, s in zip(OD_SPLITS, OD_SCALES)])
    return {
        'x': nrm(ks[0], (BATCH, SEQ, D_MODEL), 1.0),
        'ln_g': 1.0 + nrm(ks[1], (DEPTH, 2, D_MODEL), 0.02),
        'ln_b': nrm(ks[2], (DEPTH, 2, D_MODEL), 0.02),
        'ev_w_in': nrm(ks[3], (N_EVEN, D_MODEL, EV_PROJ), sd) * ev_cols,
        'ev_decay_exp': RET_DECAY_BASE + jnp.arange(RET_HEADS, dtype=F32) + nrm(ks[4], (N_EVEN, 2, RET_HEADS), 0.1),
        'ev_gn_w': 1.0 + nrm(ks[5], (N_EVEN, RET_W), 0.02),
        'ev_rpb': nrm(ks[6], (N_EVEN, NA_HEADS, 2 * NA_KH - 1, 2 * NA_KW - 1), 0.02),
        'ev_w_out': nrm(ks[7], (N_EVEN, MIX_W, D_MODEL), MIX_W ** -0.5 * DEEPNORM_BETA),
        'od_w_in': nrm(ks[8], (N_ODD, D_MODEL, OD_PROJ), sd) * od_cols,
        'od_lb': nrm(ks[9], (DEPTH, HG_F), 0.1),
        'od_norm_w': 1.0 + nrm(ks[10], (N_ODD, HG_V), 0.02),
        'od_w_out': nrm(ks[11], (N_ODD, HG_V, D_MODEL), HG_V ** -0.5 * DEEPNORM_BETA),
        'moe_w_router': nrm(ks[12], (DEPTH, D_MODEL, N_EXPERTS), sd),
        'moe_bias': nrm(ks[13], (DEPTH, N_EXPERTS), 0.01),
        'moe_w_gate': nrm(ks[14], (DEPTH, N_EXPERTS, D_MODEL, D_EXPERT), sd),
        'moe_w_up': nrm(ks[15], (DEPTH, N_EXPERTS, D_MODEL, D_EXPERT), sd),
        'moe_w_down': nrm(ks[16], (DEPTH, N_EXPERTS, D_EXPERT, D_MODEL), D_EXPERT ** -0.5 * DEEPNORM_BETA),
        'sh_w_gate': nrm(ks[17], (DEPTH, D_MODEL, D_SHARED), sd),
        'sh_w_up': nrm(ks[18], (DEPTH, D_MODEL, D_SHARED), sd),
        'sh_w_down': nrm(ks[19], (DEPTH, D_SHARED, D_MODEL), D_SHARED ** -0.5 * DEEPNORM_BETA),
    }


def reference(x, ln_g, ln_b, ev_w_in, ev_decay_exp, ev_gn_w, ev_rpb, ev_w_out, od_w_in, od_lb, od_norm_w,
              od_w_out, moe_w_router, moe_bias, moe_w_gate, moe_w_up, moe_w_down, sh_w_gate, sh_w_up, sh_w_down):
    p = jax.nn.softmax(od_lb.astype(F32), axis=0)
    lbs = jnp.cumsum(p, axis=0) - p[0]
    for layer in range(DEPTH):
        j = layer // 2
        if layer % 2 == 0:
            m = even_mixer(x, ev_w_in[j], ev_decay_exp[j], ev_gn_w[j], ev_rpb[j], ev_w_out[j])
        else:
            m = odd_mixer(x, od_w_in[j], lbs[layer], od_norm_w[j], od_w_out[j])
        x = layer_norm(DEEPNORM_ALPHA * x + m, ln_g[layer, 0], ln_b[layer, 0])
        f = moe_ffn(x, moe_w_router[layer], moe_bias[layer], moe_w_gate[layer], moe_w_up[layer],
                    moe_w_down[layer], sh_w_gate[layer], sh_w_up[layer], sh_w_down[layer])
        x = layer_norm(DEEPNORM_ALPHA * x + f, ln_g[layer, 1], ln_b[layer, 1])
    return x
```

```python
import functools

import numpy as np
import jax
import jax.numpy as jnp
from jax import lax
from jax.experimental import pallas as pl
from jax.experimental.pallas import tpu as pltpu

F32 = jnp.float32
BF16 = jnp.bfloat16

LANE = 128
SUBLANE = 8
VMEM_LIMIT = 56 * 1024 * 1024

DEPTH = 2
GRID_W = 64
RET_HEADS = 8
RET_DK = 256
RET_W = RET_HEADS * RET_DK
RET_CHUNK = 128
ROPE_BASE = 10000.0
NA_HEADS = 16
NA_DH = 128
NA_W = NA_HEADS * NA_DH
NA_KH = 8
NA_KW = 16
HG_HEADS = 32
HG_DK = 128
HG_W = HG_HEADS * HG_DK
HG_CHUNK = 256
HG_SUB = SUBLANE
N_EXPERTS = 64
TOP_K = 8
N_GROUPS = 8
TOPK_GROUPS = 4
E_PER_GROUP = N_EXPERTS // N_GROUPS
D_EXPERT = 384
ROUTED_SCALE = 2.5
MOE_BLOCK = 256
DEEPNORM_ALPHA = (2.0 * DEPTH) ** 0.25
LN_EPS = 1e-5
NORM_EPS = 1e-6
MASK_NEG = -1e30


def _params(*sem):
    return pltpu.CompilerParams(dimension_semantics=sem, vmem_limit_bytes=VMEM_LIMIT)


def _sigmoid_pair(z):
    e = jnp.exp(-jnp.abs(z))
    r = 1.0 / (1.0 + e)
    er = e * r
    pos = z >= 0
    return jnp.where(pos, r, er), jnp.where(pos, er, r)


def _silu(z):
    return z * _sigmoid_pair(z)[0]


def _mm_kernel(a_ref, w_ref, o_ref, *, heads):
    acc = jnp.dot(a_ref[...], w_ref[...], preferred_element_type=F32)
    if heads:
        for j in range(o_ref.shape[0]):
            o_ref[j] = acc[:, j * LANE:(j + 1) * LANE].astype(o_ref.dtype)
    else:
        o_ref[...] = acc.astype(o_ref.dtype)


def matmul(a, w, *, out_dtype, heads, tm=1024, tn=512):
    n, k = a.shape
    m = w.shape[1]
    tm, tn = min(tm, n), min(tn, m)
    if heads:
        out_shape = jax.ShapeDtypeStruct((m // LANE, n, LANE), out_dtype)
        out_spec = pl.BlockSpec((tn // LANE, tm, LANE), lambda i, j: (j, i, 0))
    else:
        out_shape = jax.ShapeDtypeStruct((n, m), out_dtype)
        out_spec = pl.BlockSpec((tm, tn), lambda i, j: (i, j))
    return pl.pallas_call(
        functools.partial(_mm_kernel, heads=heads),
        grid=(n // tm, m // tn),
        in_specs=[pl.BlockSpec((tm, k), lambda i, j: (i, 0)),
                  pl.BlockSpec((k, tn), lambda i, j: (0, j))],
        out_specs=out_spec,
        out_shape=out_shape,
        compiler_params=_params("arbitrary", "arbitrary"),
        name="proj_matmul",
    )(a, w)


def _layer_norm_rows(z, g, b):
    mu = jnp.mean(z, axis=-1, keepdims=True)
    zc = z - mu
    var = jnp.mean(zc * zc, axis=-1, keepdims=True)
    return zc * lax.rsqrt(var + LN_EPS) * g + b


def _ln_kernel(x_ref, m_ref, g_ref, b_ref, y_ref, yb_ref):
    z = DEEPNORM_ALPHA * x_ref[...] + m_ref[...].astype(F32)
    y = _layer_norm_rows(z, g_ref[...], b_ref[...])
    y_ref[...] = y
    yb_ref[...] = y.astype(BF16)


def ln_residual(x, m, g, b, *, tm=256):
    n, d = x.shape
    tm = min(tm, n)
    row = pl.BlockSpec((tm, d), lambda i: (i, 0))
    vec = pl.BlockSpec((1, d), lambda i: (0, 0))
    return pl.pallas_call(
        _ln_kernel,
        grid=(n // tm,),
        in_specs=[row, row, vec, vec],
        out_specs=[row, row],
        out_shape=[jax.ShapeDtypeStruct((n, d), F32), jax.ShapeDtypeStruct((n, d), BF16)],
        compiler_params=_params("arbitrary"),
        name="ln_residual",
    )(x, m, g.reshape(1, d), b.reshape(1, d))


def _ret_kernel(lg_ref, q_ref, k_ref, v_ref, cos_ref, sin_ref, *rest, rev, n_chunks):
    if rev:
        oa_ref, g_ref, gn_ref, out_ref, s_ref = rest
    else:
        out_ref, s_ref = rest
    C = RET_CHUNK
    h = pl.program_id(1)

    @pl.when(pl.program_id(2) == 0)
    def _():
        s_ref[...] = jnp.zeros_like(s_ref)

    lg = lg_ref[1 if rev else 0, h]
    row = lax.broadcasted_iota(jnp.int32, (C, RET_DK), 0).astype(F32)
    if rev:
        q_dec = jnp.exp(lg * (C - row))
        k_dec = jnp.exp(lg * row)
    else:
        q_dec = jnp.exp(lg * (row + 1.0))
        k_dec = jnp.exp(lg * (C - 1.0 - row))
        dist = (lax.broadcasted_iota(jnp.int32, (C, C), 0)
                - lax.broadcasted_iota(jnp.int32, (C, C), 1)).astype(F32)
        decay = jnp.where(dist >= 0,
                          jnp.exp(lg_ref[0, h] * jnp.maximum(dist, 0.0)),
                          jnp.exp(lg_ref[1, h] * jnp.maximum(-dist, 0.0)))
    chunk_decay = jnp.exp(jnp.full((1, 1), C, F32) * lg)

    def rot(ref, sl, cos, sin):
        t1 = ref[0, sl, :].astype(F32)
        t2 = ref[1, sl, :].astype(F32)
        return jnp.concatenate([t1 * cos - t2 * sin, t1 * sin + t2 * cos], axis=1)

    order = range(n_chunks - 1, -1, -1) if rev else range(n_chunks)
    for ci in order:
        sl = slice(ci * C, (ci + 1) * C)
        cos, sin = cos_ref[sl, :], sin_ref[sl, :]
        qr = rot(q_ref, sl, cos, sin)
        kr = rot(k_ref, sl, cos, sin) * (RET_DK ** -0.5)
        v = jnp.concatenate([v_ref[0, sl, :], v_ref[1, sl, :]], axis=1)
        state = s_ref[...]
        o = jnp.dot((qr * q_dec).astype(BF16), state.astype(BF16), preferred_element_type=F32)
        if rev:
            o = o + jnp.concatenate([oa_ref[0, sl, :], oa_ref[1, sl, :]], axis=1)
            mu = jnp.mean(o, axis=-1, keepdims=True)
            oc = o - mu
            var = jnp.mean(oc * oc, axis=-1, keepdims=True)
            gate = jnp.concatenate([g_ref[0, sl, :], g_ref[1, sl, :]], axis=1).astype(F32)
            res = oc * lax.rsqrt(var + NORM_EPS) * gn_ref[...] * _silu(gate)
            out_ref[sl, :] = res.astype(out_ref.dtype)
        else:
            scores = lax.dot_general(qr.astype(BF16), kr.astype(BF16), (((1,), (1,)), ((), ())),
                                     preferred_element_type=F32) * decay
            o = o + jnp.dot(scores.astype(BF16), v, preferred_element_type=F32)
            out_ref[0, sl, :] = o[:, :LANE]
            out_ref[1, sl, :] = o[:, LANE:]
        kv = lax.dot_general((kr * k_dec).astype(BF16), v, (((0,), (0,)), ((), ())),
                             preferred_element_type=F32)
        s_ref[...] = chunk_decay * state + kv


def _ret_pass(lg, ph, cos, sin, extra, *, rev, batch, seq, rows):
    n_blk = seq // rows
    n_chunks = rows // RET_CHUNK
    H = RET_HEADS

    def blk(step):
        return (n_blk - 1 - step) if rev else step

    def slab(group):
        return pl.BlockSpec((2, rows, LANE), lambda b, h, s: (group * H + h, b * n_blk + blk(s), 0))

    tab = pl.BlockSpec((rows, LANE), lambda b, h, s: (blk(s), 0))
    in_specs = [pl.BlockSpec(memory_space=pltpu.SMEM), slab(0), slab(1), slab(2), tab, tab]
    args = [lg, ph, ph, ph, cos, sin]
    if rev:
        oa, gn_w = extra
        in_specs += [pl.BlockSpec((2, rows, LANE), lambda b, h, s: (h, b * n_blk + blk(s), 0)),
                     slab(3),
                     pl.BlockSpec((1, RET_DK), lambda b, h, s: (0, h))]
        args += [oa, ph, gn_w.reshape(1, RET_W)]
        out_shape = jax.ShapeDtypeStruct((batch * seq, RET_W), BF16)
        out_spec = pl.BlockSpec((rows, RET_DK), lambda b, h, s: (b * n_blk + blk(s), h))
    else:
        out_shape = jax.ShapeDtypeStruct((2 * H, batch * seq, LANE), F32)
        out_spec = pl.BlockSpec((2, rows, LANE), lambda b, h, s: (h, b * n_blk + blk(s), 0))
    return pl.pallas_call(
        functools.partial(_ret_kernel, rev=rev, n_chunks=n_chunks),
        grid=(batch, H, n_blk),
        in_specs=in_specs,
        out_specs=out_spec,
        out_shape=out_shape,
        scratch_shapes=[pltpu.VMEM((RET_DK, RET_DK), F32)],
        compiler_params=_params("arbitrary", "arbitrary", "arbitrary"),
        name="retention_bwd" if rev else "retention_fwd",
    )(*args)


def retention(ph, decay_exp, gn_w, *, batch, seq, rows=512):
    rows = min(rows, seq)
    log_g = jnp.log1p(-jnp.exp2(-decay_exp.astype(F32)))
    inv = ROPE_BASE ** (-jnp.linspace(0.0, 1.0, RET_DK // 2, dtype=F32))
    ang = jnp.arange(seq, dtype=F32)[:, None] * inv[None, :]
    cos, sin = jnp.cos(ang), jnp.sin(ang)
    o_a = _ret_pass(log_g, ph, cos, sin, None, rev=False, batch=batch, seq=seq, rows=rows)
    return _ret_pass(log_g, ph, cos, sin, (o_a, gn_w), rev=True, batch=batch, seq=seq, rows=rows)


def _na_kernel(q_ref, k_ref, v_ref, tab_ref, o_ref, *, grid_rows):
    span = NA_KH * GRID_W
    scale = NA_DH ** -0.5

    def body(r, carry):
        r0 = jnp.clip(r - NA_KH // 2, 0, grid_rows - NA_KH)
        q = q_ref[0, pl.ds(pl.multiple_of(r * GRID_W, GRID_W), GRID_W), :]
        k0 = pl.multiple_of(r0 * GRID_W, GRID_W)
        kw = k_ref[0, pl.ds(k0, span), :]
        vw = v_ref[0, pl.ds(k0, span), :]
        s = lax.dot_general(q, kw, (((1,), (1,)), ((), ())), preferred_element_type=F32)
        s = s * scale + tab_ref[0, r0 - r + NA_KH - 1]
        m = jnp.max(s, axis=-1, keepdims=True)
        p = jnp.exp(s - m)
        l = jnp.sum(p, axis=-1, keepdims=True)
        o = jnp.dot(p.astype(BF16), vw, preferred_element_type=F32) / l
        o_ref[pl.ds(pl.multiple_of(r * GRID_W, GRID_W), GRID_W), :] = o.astype(o_ref.dtype)
        return carry

    lax.fori_loop(0, grid_rows, body, 0)


def _na_bias_table(rpb):
    qc = np.arange(GRID_W)[:, None]
    kc = np.arange(GRID_W)[None, :]
    wstart = np.clip(qc - NA_KW // 2, 0, GRID_W - NA_KW)
    valid = (kc >= wstart) & (kc < wstart + NA_KW)
    dc = np.clip(kc - qc, -(NA_KW - 1), NA_KW - 1) + NA_KW - 1
    dr = np.arange(NA_KH)[:, None] + np.arange(NA_KH)[None, :]
    tab = rpb[:, dr[:, :, None, None], dc[None, None, :, :]]
    tab = jnp.where(jnp.asarray(valid)[None, None, None], tab, MASK_NEG)
    return tab.transpose(0, 1, 3, 2, 4).reshape(NA_HEADS, NA_KH, GRID_W, NA_KH * GRID_W)


def neighborhood_attention(ph, rpb, *, batch, seq, slab0):
    grid_rows = seq // GRID_W
    assert grid_rows >= NA_KH
    tab = _na_bias_table(rpb.astype(F32))

    def slab(group):
        return pl.BlockSpec((1, seq, LANE), lambda b, h: (slab0 + group * NA_HEADS + h, b, 0))

    return pl.pallas_call(
        functools.partial(_na_kernel, grid_rows=grid_rows),
        grid=(batch, NA_HEADS),
        in_specs=[slab(0), slab(1), slab(2),
                  pl.BlockSpec((1, NA_KH, GRID_W, NA_KH * GRID_W), lambda b, h: (h, 0, 0, 0))],
        out_specs=pl.BlockSpec((seq, NA_DH), lambda b, h: (b, h)),
        out_shape=jax.ShapeDtypeStruct((batch * seq, NA_W), BF16),
        compiler_params=_params("arbitrary", "arbitrary"),
        name="neighborhood_attention",
    )(ph, ph, ph, tab)


def _hg_kernel(lb_ref, zq_ref, zf_ref, zi_ref, *rest, rev, layer, n_chunks):
    if rev:
        of_ref, zg_ref, nw_ref, out_ref, st_ref = rest
    else:
        out_ref, st_ref = rest
    C = HG_CHUNK
    U = HG_SUB

    @pl.when(pl.program_id(2) == 0)
    def _():
        st_ref[...] = jnp.zeros_like(st_ref)

    lbp = lb_ref[...]
    pe = jnp.exp(lbp - jnp.max(lbp, axis=0, keepdims=True))
    share = pe / jnp.sum(pe, axis=0, keepdims=True)
    lb = jnp.sum(share[:layer + 1], axis=0, keepdims=True) - share[0:1]

    ri = lax.broadcasted_iota(jnp.int32, (C, C), 0)
    ci_ = lax.broadcasted_iota(jnp.int32, (C, C), 1)
    tri = jnp.where((ri <= ci_) if rev else (ri >= ci_), 1.0, 0.0).astype(BF16)

    order = range(n_chunks - 1, -1, -1) if rev else range(n_chunks)
    for ci in order:
        sl = slice(ci * C, (ci + 1) * C)
        q = _silu(zq_ref[0, sl, :].astype(F32))
        sig, nsig = _sigmoid_pair(zf_ref[0, sl, :].astype(F32))
        k = (1.0 - lb) * nsig
        lf = jnp.log(lb + (1.0 - lb) * sig)
        v = zi_ref[0, sl, :]

        hi = lf.astype(BF16)
        r1 = lf - hi.astype(F32)
        mid = r1.astype(BF16)
        lo = (r1 - mid.astype(F32)).astype(BF16)
        parts = jnp.dot(tri, jnp.concatenate([hi, mid, lo], axis=1), preferred_element_type=F32)
        b = parts[:, :LANE] + parts[:, LANE:2 * LANE] + parts[:, 2 * LANE:]

        state_t = st_ref[...]
        o = lax.dot_general((q * jnp.exp(b)).astype(BF16), state_t.astype(BF16),
                            (((1,), (1,)), ((), ())), preferred_element_type=F32)

        att = jnp.zeros((C, C), F32)
        w = C // 2
        while w >= U:
            nb = C // (2 * w)
            b3 = b.reshape(nb, 2 * w, HG_DK)
            pos = lax.broadcasted_iota(jnp.int32, (nb, 2 * w, HG_DK), 1)
            if rev:
                is_q = pos < w
                edge = b3[:, w:w + 1, :]
            else:
                is_q = pos >= w
                edge = b3[:, w - 1:w, :]
            d = b3 - edge
            e = jnp.exp(jnp.where(is_q, d, -d)).reshape(C, HG_DK)
            is_q = is_q.reshape(C, HG_DK)
            qh = jnp.where(is_q, q * e, 0.0).astype(BF16)
            kh = jnp.where(is_q, 0.0, k * e).astype(BF16)
            a = lax.dot_general(qh, kh, (((1,), (1,)), ((), ())), preferred_element_type=F32)
            att = att + jnp.where((ri // (2 * w)) == (ci_ // (2 * w)), a, 0.0)
            w //= 2
        o = o + jnp.dot(att.astype(BF16), v, preferred_element_type=F32)

        nu = C // U
        b8 = b.reshape(nu, U, HG_DK)
        q8 = q.reshape(nu, U, HG_DK)
        k8 = k.reshape(nu, U, HG_DK)
        v8 = v.astype(F32).reshape(nu, U, HG_DK)
        sidx = lax.broadcasted_iota(jnp.int32, (nu, U, HG_DK), 1)
        od = jnp.zeros((nu, U, HG_DK), F32)
        for t in range(U):
            seen = (sidx >= t) if rev else (sidx <= t)
            e = jnp.exp(jnp.where(seen, b8[:, t:t + 1, :] - b8, MASK_NEG))
            a = jnp.sum(e * k8 * q8[:, t:t + 1, :], axis=-1, keepdims=True)
            row = jnp.sum(a * v8, axis=1, keepdims=True)
            od = jnp.where(sidx == t, row, od)
        o = o + od.reshape(C, HG_DK)

        b_end = b[0:1, :] if rev else b[C - 1:C, :]
        kd = (k * jnp.exp(b_end - b)).astype(BF16)
        st_ref[...] = state_t * jnp.exp(b_end) + lax.dot_general(
            v, kd, (((0,), (0,)), ((), ())), preferred_element_type=F32)

        if rev:
            o = o + of_ref[0, sl, :]
            o = o * lax.rsqrt(jnp.mean(o * o, axis=-1, keepdims=True) + NORM_EPS)
            res = o * nw_ref[...] * _silu(zg_ref[0, sl, :].astype(F32))
            out_ref[sl, :] = res.astype(out_ref.dtype)
        else:
            out_ref[0, sl, :] = o


def _hg_pass(od_lb, ph, extra, *, rev, layer, batch, seq, rows):
    n_blk = seq // rows
    H = HG_HEADS

    def blk(step):
        return (n_blk - 1 - step) if rev else step

    def slab(group):
        return pl.BlockSpec((1, rows, LANE), lambda b, h, s: (group * H + h, b * n_blk + blk(s), 0))

    in_specs = [pl.BlockSpec((od_lb.shape[0], HG_DK), lambda b, h, s: (0, h)),
                slab(0), slab(2 if rev else 1), slab(3)]
    args = [od_lb, ph, ph, ph]
    if rev:
        o_f, norm_w = extra
        in_specs += [pl.BlockSpec((1, rows, LANE), lambda b, h, s: (h, b * n_blk + blk(s), 0)),
                     slab(4),
                     pl.BlockSpec((1, HG_DK), lambda b, h, s: (0, h))]
        args += [o_f, ph, norm_w.reshape(1, HG_W)]
        out_shape = jax.ShapeDtypeStruct((batch * seq, HG_W), BF16)
        out_spec = pl.BlockSpec((rows, HG_DK), lambda b, h, s: (b * n_blk + blk(s), h))
    else:
        out_shape = jax.ShapeDtypeStruct((H, batch * seq, LANE), F32)
        out_spec = pl.BlockSpec((1, rows, LANE), lambda b, h, s: (h, b * n_blk + blk(s), 0))
    return pl.pallas_call(
        functools.partial(_hg_kernel, rev=rev, layer=layer, n_chunks=rows // HG_CHUNK),
        grid=(batch, H, n_blk),
        in_specs=in_specs,
        out_specs=out_spec,
        out_shape=out_shape,
        scratch_shapes=[pltpu.VMEM((HG_DK, HG_DK), F32)],
        compiler_params=_params("arbitrary", "arbitrary", "arbitrary"),
        name="hgrn2_bwd" if rev else "hgrn2_fwd",
    )(*args)


def hgrn2(ph, od_lb, norm_w, *, layer, batch, seq, rows=HG_CHUNK):
    od_lb = od_lb.astype(F32)
    o_f = _hg_pass(od_lb, ph, None, rev=False, layer=layer, batch=batch, seq=seq, rows=rows)
    return _hg_pass(od_lb, ph, (o_f, norm_w), rev=True, layer=layer, batch=batch, seq=seq, rows=rows)


def _router_kernel(wt_ref, x_ref, s_ref):
    logits = lax.dot_general(wt_ref[...], x_ref[...], (((1,), (1,)), ((), ())),
                             precision=lax.Precision.HIGHEST, preferred_element_type=F32)
    s_ref[...] = _sigmoid_pair(logits)[0]


def router_scores(x, w_router, *, tm=512):
    n, d = x.shape
    tm = min(tm, n)
    return pl.pallas_call(
        _router_kernel,
        grid=(n // tm,),
        in_specs=[pl.BlockSpec((N_EXPERTS, d), lambda i: (0, 0)),
                  pl.BlockSpec((tm, d), lambda i: (i, 0))],
        out_specs=pl.BlockSpec((N_EXPERTS, tm), lambda i: (0, i)),
        out_shape=jax.ShapeDtypeStruct((N_EXPERTS, n), F32),
        compiler_params=_params("arbitrary"),
        name="router",
    )(w_router.T, x)


def _expert_kernel(be_ref, nu_ref, x_ref, g_ref, wg_ref, wu_ref, wd_ref, y_ref):
    i = pl.program_id(0)

    @pl.when(i < nu_ref[0])
    def _():
        x = x_ref[...]
        hg = jnp.dot(x, wg_ref[0], preferred_element_type=F32)
        hu = jnp.dot(x, wu_ref[0], preferred_element_type=F32)
        hid = (_silu(hg) * hu).astype(BF16)
        y = jnp.dot(hid, wd_ref[0], preferred_element_type=F32)
        y_ref[...] = (y * g_ref[...]).astype(y_ref.dtype)

    @pl.when(i >= nu_ref[0])
    def _():
        y_ref[...] = jnp.zeros_like(y_ref)


def expert_blocks(xg, gate, blk_e, n_used, w_gate, w_up, w_down):
    n_rows, d = xg.shape
    n_blocks = n_rows // MOE_BLOCK
    grid_spec = pltpu.PrefetchScalarGridSpec(
        num_scalar_prefetch=2,
        grid=(n_blocks,),
        in_specs=[pl.BlockSpec((MOE_BLOCK, d), lambda i, be, nu: (i, 0)),
                  pl.BlockSpec((MOE_BLOCK, 1), lambda i, be, nu: (i, 0)),
                  pl.BlockSpec((1, d, D_EXPERT), lambda i, be, nu: (be[i], 0, 0)),
                  pl.BlockSpec((1, d, D_EXPERT), lambda i, be, nu: (be[i], 0, 0)),
                  pl.BlockSpec((1, D_EXPERT, d), lambda i, be, nu: (be[i], 0, 0))],
        out_specs=pl.BlockSpec((MOE_BLOCK, d), lambda i, be, nu: (i, 0)),
    )
    return pl.pallas_call(
        _expert_kernel,
        grid_spec=grid_spec,
        out_shape=jax.ShapeDtypeStruct((n_rows, d), BF16),
        compiler_params=_params("arbitrary"),
        name="expert_blocks",
    )(blk_e, n_used, xg, gate.reshape(n_rows, 1), w_gate, w_up, w_down)


def _combine_kernel(x_ref, xb_ref, yk_ref, sg_ref, su_ref, sd_ref, g_ref, b_ref, y_ref, yb_ref):
    xb = xb_ref[...]
    hg = jnp.dot(xb, sg_ref[...], preferred_element_type=F32)
    hu = jnp.dot(xb, su_ref[...], preferred_element_type=F32)
    hid = (_silu(hg) * hu).astype(BF16)
    f = jnp.dot(hid, sd_ref[...], preferred_element_type=F32)
    for j in range(TOP_K):
        f = f + yk_ref[j].astype(F32)
    y = _layer_norm_rows(DEEPNORM_ALPHA * x_ref[...] + f, g_ref[...], b_ref[...])
    y_ref[...] = y
    yb_ref[...] = y.astype(BF16)


def moe_combine(x, xb, yk, s_gate, s_up, s_down, g, b, *, tm=128):
    n, d = x.shape
    tm = min(tm, n)
    row = pl.BlockSpec((tm, d), lambda i: (i, 0))
    vec = pl.BlockSpec((1, d), lambda i: (0, 0))
    return pl.pallas_call(
        _combine_kernel,
        grid=(n // tm,),
        in_specs=[row, row, pl.BlockSpec((TOP_K, tm, d), lambda i: (0, i, 0)),
                  pl.BlockSpec((d, D_EXPERT), lambda i: (0, 0)),
                  pl.BlockSpec((d, D_EXPERT), lambda i: (0, 0)),
                  pl.BlockSpec((D_EXPERT, d), lambda i: (0, 0)), vec, vec],
        out_specs=[row, row],
        out_shape=[jax.ShapeDtypeStruct((n, d), F32), jax.ShapeDtypeStruct((n, d), BF16)],
        compiler_params=_params("arbitrary"),
        name="moe_combine",
    )(x, xb, yk, s_gate, s_up, s_down, g.reshape(1, d), b.reshape(1, d))


def moe_ffn(x, xb, w_router, r_bias, w_gate, w_up, w_down, s_gate, s_up, s_down, ln_g, ln_b):
    n_tok, d = x.shape
    scores = router_scores(x, w_router).T
    sel = scores + r_bias.astype(F32)
    grp_score = lax.top_k(sel.reshape(n_tok, N_GROUPS, E_PER_GROUP), 2)[0].sum(-1)
    _, top_grp = lax.top_k(grp_score, TOPK_GROUPS)
    grp_ok = jnp.any(top_grp[:, :, None] == jnp.arange(N_GROUPS)[None, None, :], axis=1)
    sel = jnp.where(jnp.repeat(grp_ok, E_PER_GROUP, axis=1), sel, -jnp.inf)
    _, idx = lax.top_k(sel, TOP_K)
    gw = jnp.take_along_axis(scores, idx, axis=1)
    gw = gw / jnp.sum(gw, -1, keepdims=True) * ROUTED_SCALE
    n_assign = n_tok * TOP_K
    e_flat = idx.reshape(n_assign)
    t_flat = jnp.repeat(jnp.arange(n_tok, dtype=jnp.int32), TOP_K)
    g_flat = gw.reshape(n_assign)
    order = jnp.argsort(e_flat)
    e_s, t_s, g_s = e_flat[order], t_flat[order], g_flat[order]
    counts = jnp.bincount(e_flat, length=N_EXPERTS)
    starts = jnp.cumsum(counts) - counts
    pcounts = (counts + MOE_BLOCK - 1) // MOE_BLOCK * MOE_BLOCK
    pends = jnp.cumsum(pcounts)
    pstarts = pends - pcounts
    dest = (pstarts[e_s] + jnp.arange(n_assign, dtype=jnp.int32) - starts[e_s]).astype(jnp.int32)
    n_blocks = -(-n_assign // MOE_BLOCK) + N_EXPERTS
    n_rows = n_blocks * MOE_BLOCK
    tok_buf = jnp.zeros((n_rows,), jnp.int32).at[dest].set(t_s)
    gate_buf = jnp.zeros((n_rows,), F32).at[dest].set(g_s)
    blk_e = jnp.minimum(jnp.searchsorted(pends, jnp.arange(n_blocks, dtype=jnp.int32) * MOE_BLOCK, side='right'),
                        N_EXPERTS - 1).astype(jnp.int32)
    n_used = (pends[-1] // MOE_BLOCK).astype(jnp.int32).reshape(1)
    pos = jnp.zeros((n_assign,), jnp.int32).at[order].set(dest).reshape(n_tok, TOP_K)

    xg = xb[tok_buf]
    yg = expert_blocks(xg, gate_buf, blk_e, n_used, w_gate, w_up, w_down)
    yk = yg[pos.T]
    return moe_combine(x, xb, yk, s_gate, s_up, s_down, ln_g, ln_b)


def kernel(x, ln_g, ln_b, ev_w_in, ev_decay_exp, ev_gn_w, ev_rpb, ev_w_out, od_w_in, od_lb, od_norm_w, od_w_out, moe_w_router, moe_bias, moe_w_gate, moe_w_up, moe_w_down, sh_w_gate, sh_w_up, sh_w_down):
    batch, seq, d = x.shape
    n = batch * seq
    xf = x.reshape(n, d).astype(F32)
    xb = xf.astype(BF16)
    for layer in range(DEPTH):
        j = layer // 2
        if layer % 2 == 0:
            ph = matmul(xb, ev_w_in[j].astype(BF16), out_dtype=BF16, heads=True)
            o_ret = retention(ph, ev_decay_exp[j], ev_gn_w[j].astype(F32), batch=batch, seq=seq)
            o_na = neighborhood_attention(ph, ev_rpb[j], batch=batch, seq=seq, slab0=4 * RET_W // LANE)
            mixed = jnp.concatenate([o_ret, o_na], axis=1)
            w_out = ev_w_out[j]
        else:
            ph = matmul(xb, od_w_in[j].astype(BF16), out_dtype=BF16, heads=True)
            mixed = hgrn2(ph, od_lb, od_norm_w[j].astype(F32), layer=layer, batch=batch, seq=seq)
            w_out = od_w_out[j]
        m = matmul(mixed, w_out.astype(BF16), out_dtype=F32, heads=False)
        xf, xb = ln_residual(xf, m, ln_g[layer, 0].astype(F32), ln_b[layer, 0].astype(F32))
        xf, xb = moe_ffn(xf, xb, moe_w_router[layer].astype(F32), moe_bias[layer],
                         moe_w_gate[layer].astype(BF16), moe_w_up[layer].astype(BF16),
                         moe_w_down[layer].astype(BF16), sh_w_gate[layer].astype(BF16),
                         sh_w_up[layer].astype(BF16), sh_w_down[layer].astype(BF16),
                         ln_g[layer, 1].astype(F32), ln_b[layer, 1].astype(F32))
    return xf.reshape(batch, seq, d)
```

```python
import functools

import numpy as np
import jax
import jax.numpy as jnp
from jax import lax
from jax.experimental import pallas as pl
from jax.experimental.pallas import tpu as pltpu

F32 = jnp.float32
BF16 = jnp.bfloat16
U32 = jnp.uint32
I32 = jnp.int32

LANE = 128
SUBLANE = 8
VMEM_LIMIT = 56 * 1024 * 1024

DEPTH = 2
GRID_W = 64
RET_HEADS = 8
RET_DK = 256
RET_W = RET_HEADS * RET_DK
RET_CHUNK = 128
ROPE_BASE = 10000.0
NA_HEADS = 16
NA_DH = 128
NA_W = NA_HEADS * NA_DH
NA_KH = 8
NA_KW = 16
NA_UNROLL = 8
HG_HEADS = 32
HG_DK = 128
HG_W = HG_HEADS * HG_DK
HG_CHUNK = 128
HG_SUB = SUBLANE
HG_ROWS = 1024
HG_UNROLL = 4
N_EXPERTS = 64
TOP_K = 8
N_GROUPS = 8
TOPK_GROUPS = 4
E_PER_GROUP = N_EXPERTS // N_GROUPS
D_EXPERT = 384
ROUTED_SCALE = 2.5
MOE_BLOCK = 256
ROUTER_TILE = 512
DISPATCH_TILE = 256
COMBINE_TILE = 128
DEEPNORM_ALPHA = (2.0 * DEPTH) ** 0.25
LN_EPS = 1e-5
NORM_EPS = 1e-6
MASK_NEG = -1e30


def _params(*sem):
    return pltpu.CompilerParams(dimension_semantics=sem, vmem_limit_bytes=VMEM_LIMIT)


def _sigmoid_pair(z):
    e = jnp.exp(-jnp.abs(z))
    r = 1.0 / (1.0 + e)
    er = e * r
    pos = z >= 0
    return jnp.where(pos, r, er), jnp.where(pos, er, r)


def _silu(z):
    return z * _sigmoid_pair(z)[0]


def _pack_rows(y):
    half = y.shape[1] // 2
    lo = lax.bitcast_convert_type(y[:, :half].astype(BF16).astype(F32), U32) >> 16
    hi = lax.bitcast_convert_type(y[:, half:].astype(BF16).astype(F32), U32) & jnp.uint32(0xFFFF0000)
    return hi | lo


def _unpack_rows(u):
    lo = lax.bitcast_convert_type(u << 16, F32)
    hi = lax.bitcast_convert_type(u & jnp.uint32(0xFFFF0000), F32)
    return lo, hi


def _mm_kernel(a_ref, w_ref, o_ref, wb_ref, *, heads):
    @pl.when(pl.program_id(1) == 0)
    def _():
        wb_ref[...] = w_ref[...].astype(BF16)

    acc = jnp.dot(a_ref[...], wb_ref[...], preferred_element_type=F32)
    if heads:
        for j in range(o_ref.shape[0]):
            o_ref[j] = acc[:, j * LANE:(j + 1) * LANE].astype(o_ref.dtype)
    else:
        o_ref[...] = acc.astype(o_ref.dtype)


def matmul(a, w, layer, *, out_dtype, heads, tm=1024, tn=512):
    n, k = a.shape
    m = w.shape[2]
    tm, tn = min(tm, n), min(tn, m)
    if heads:
        out_shape = jax.ShapeDtypeStruct((m // LANE, n, LANE), out_dtype)
        out_spec = pl.BlockSpec((tn // LANE, tm, LANE), lambda j, i: (j, i, 0))
    else:
        out_shape = jax.ShapeDtypeStruct((n, m), out_dtype)
        out_spec = pl.BlockSpec((tm, tn), lambda j, i: (i, j))
    return pl.pallas_call(
        functools.partial(_mm_kernel, heads=heads),
        grid=(m // tn, n // tm),
        in_specs=[pl.BlockSpec((tm, k), lambda j, i: (i, 0)),
                  pl.BlockSpec((None, k, tn), lambda j, i: (layer, 0, j))],
        out_specs=out_spec,
        out_shape=out_shape,
        scratch_shapes=[pltpu.VMEM((k, tn), BF16)],
        compiler_params=_params("arbitrary", "arbitrary"),
        name="proj_matmul",
    )(a, w)


def _layer_norm_rows(z, g, b):
    mu = jnp.mean(z, axis=-1, keepdims=True)
    zc = z - mu
    var = jnp.mean(zc * zc, axis=-1, keepdims=True)
    return zc * lax.rsqrt(var + LN_EPS) * g + b


def _ln_kernel(x_ref, m_ref, g_ref, b_ref, y_ref, yb_ref, yp_ref):
    z = DEEPNORM_ALPHA * x_ref[...] + m_ref[...].astype(F32)
    y = _layer_norm_rows(z, g_ref[...], b_ref[...])
    y_ref[...] = y
    yb_ref[...] = y.astype(BF16)
    yp_ref[...] = _pack_rows(y)


def ln_residual(x, m, g, b, *, tm=256):
    n, d = x.shape
    tm = min(tm, n)
    row = pl.BlockSpec((tm, d), lambda i: (i, 0))
    prow = pl.BlockSpec((tm, d // 2), lambda i: (i, 0))
    vec = pl.BlockSpec((1, d), lambda i: (0, 0))
    return pl.pallas_call(
        _ln_kernel,
        grid=(n // tm,),
        in_specs=[row, row, vec, vec],
        out_specs=[row, row, prow],
        out_shape=[jax.ShapeDtypeStruct((n, d), F32), jax.ShapeDtypeStruct((n, d), BF16),
                   jax.ShapeDtypeStruct((n, d // 2), U32)],
        compiler_params=_params("arbitrary"),
        name="ln_residual",
    )(x, m, g.reshape(1, d), b.reshape(1, d))


def _ret_kernel(lg_ref, q_ref, k_ref, v_ref, cos_ref, sin_ref, *rest, rev, n_chunks):
    if rev:
        oa_ref, g_ref, gn_ref, out_ref, s_ref = rest
    else:
        out_ref, s_ref = rest
    C = RET_CHUNK
    h = pl.program_id(1)

    @pl.when(pl.program_id(2) == 0)
    def _():
        s_ref[...] = jnp.zeros_like(s_ref)

    lg = lg_ref[1 if rev else 0, h]
    row = lax.broadcasted_iota(I32, (C, RET_DK), 0).astype(F32)
    if rev:
        q_dec = jnp.exp(lg * (C - row))
        k_dec = jnp.exp(lg * row)
    else:
        q_dec = jnp.exp(lg * (row + 1.0))
        k_dec = jnp.exp(lg * (C - 1.0 - row))
        dist = (lax.broadcasted_iota(I32, (C, C), 0)
                - lax.broadcasted_iota(I32, (C, C), 1)).astype(F32)
        decay = jnp.where(dist >= 0,
                          jnp.exp(lg_ref[0, h] * jnp.maximum(dist, 0.0)),
                          jnp.exp(lg_ref[1, h] * jnp.maximum(-dist, 0.0)))
    chunk_decay = jnp.exp(jnp.full((1, 1), C, F32) * lg)

    def rot(ref, sl, cos, sin):
        t1 = ref[0, sl, :].astype(F32)
        t2 = ref[1, sl, :].astype(F32)
        return jnp.concatenate([t1 * cos - t2 * sin, t1 * sin + t2 * cos], axis=1)

    order = range(n_chunks - 1, -1, -1) if rev else range(n_chunks)
    for ci in order:
        sl = slice(ci * C, (ci + 1) * C)
        cos, sin = cos_ref[sl, :], sin_ref[sl, :]
        qr = rot(q_ref, sl, cos, sin)
        kr = rot(k_ref, sl, cos, sin) * (RET_DK ** -0.5)
        v = jnp.concatenate([v_ref[0, sl, :], v_ref[1, sl, :]], axis=1)
        state = s_ref[...]
        o = jnp.dot((qr * q_dec).astype(BF16), state.astype(BF16), preferred_element_type=F32)
        if rev:
            o = o + jnp.concatenate([oa_ref[0, sl, :], oa_ref[1, sl, :]], axis=1)
            mu = jnp.mean(o, axis=-1, keepdims=True)
            oc = o - mu
            var = jnp.mean(oc * oc, axis=-1, keepdims=True)
            gate = jnp.concatenate([g_ref[0, sl, :], g_ref[1, sl, :]], axis=1).astype(F32)
            res = oc * lax.rsqrt(var + NORM_EPS) * gn_ref[...] * _silu(gate)
            out_ref[sl, :] = res.astype(out_ref.dtype)
        else:
            scores = lax.dot_general(qr.astype(BF16), kr.astype(BF16), (((1,), (1,)), ((), ())),
                                     preferred_element_type=F32) * decay
            o = o + jnp.dot(scores.astype(BF16), v, preferred_element_type=F32)
            out_ref[0, sl, :] = o[:, :LANE]
            out_ref[1, sl, :] = o[:, LANE:]
        kv = lax.dot_general((kr * k_dec).astype(BF16), v, (((0,), (0,)), ((), ())),
                             preferred_element_type=F32)
        s_ref[...] = chunk_decay * state + kv


def _ret_pass(lg, ph, cos, sin, extra, *, rev, batch, seq, rows):
    n_blk = seq // rows
    n_chunks = rows // RET_CHUNK
    H = RET_HEADS

    def blk(step):
        return (n_blk - 1 - step) if rev else step

    def slab(group):
        return pl.BlockSpec((2, rows, LANE), lambda b, h, s: (group * H + h, b * n_blk + blk(s), 0))

    tab = pl.BlockSpec((rows, LANE), lambda b, h, s: (blk(s), 0))
    in_specs = [pl.BlockSpec(memory_space=pltpu.SMEM), slab(0), slab(1), slab(2), tab, tab]
    args = [lg, ph, ph, ph, cos, sin]
    if rev:
        oa, gn_w = extra
        in_specs += [pl.BlockSpec((2, rows, LANE), lambda b, h, s: (h, b * n_blk + blk(s), 0)),
                     slab(3),
                     pl.BlockSpec((1, RET_DK), lambda b, h, s: (0, h))]
        args += [oa, ph, gn_w.reshape(1, RET_W)]
        out_shape = jax.ShapeDtypeStruct((batch * seq, RET_W), BF16)
        out_spec = pl.BlockSpec((rows, RET_DK), lambda b, h, s: (b * n_blk + blk(s), h))
    else:
        out_shape = jax.ShapeDtypeStruct((2 * H, batch * seq, LANE), F32)
        out_spec = pl.BlockSpec((2, rows, LANE), lambda b, h, s: (h, b * n_blk + blk(s), 0))
    return pl.pallas_call(
        functools.partial(_ret_kernel, rev=rev, n_chunks=n_chunks),
        grid=(batch, H, n_blk),
        in_specs=in_specs,
        out_specs=out_spec,
        out_shape=out_shape,
        scratch_shapes=[pltpu.VMEM((RET_DK, RET_DK), F32)],
        compiler_params=_params("arbitrary", "arbitrary", "arbitrary"),
        name="retention_bwd" if rev else "retention_fwd",
    )(*args)


def retention(ph, decay_exp, gn_w, *, batch, seq, rows=512):
    rows = min(rows, seq)
    log_g = jnp.log1p(-jnp.exp2(-decay_exp.astype(F32)))
    inv = ROPE_BASE ** (-jnp.linspace(0.0, 1.0, RET_DK // 2, dtype=F32))
    ang = jnp.arange(seq, dtype=F32)[:, None] * inv[None, :]
    cos, sin = jnp.cos(ang), jnp.sin(ang)
    o_a = _ret_pass(log_g, ph, cos, sin, None, rev=False, batch=batch, seq=seq, rows=rows)
    return _ret_pass(log_g, ph, cos, sin, (o_a, gn_w), rev=True, batch=batch, seq=seq, rows=rows)


def _na_kernel(q_ref, k_ref, v_ref, tab_ref, o_ref, *, grid_rows):
    span = NA_KH * GRID_W
    scale = NA_DH ** -0.5

    def body(it, carry):
        rows = [it * NA_UNROLL + j for j in range(NA_UNROLL)]
        starts = [jnp.clip(r - NA_KH // 2, 0, grid_rows - NA_KH) for r in rows]
        scores = []
        for r, r0 in zip(rows, starts):
            q = q_ref[0, pl.ds(pl.multiple_of(r * GRID_W, GRID_W), GRID_W), :]
            kw = k_ref[0, pl.ds(pl.multiple_of(r0 * GRID_W, GRID_W), span), :]
            s = lax.dot_general(q, kw, (((1,), (1,)), ((), ())), preferred_element_type=F32)
            scores.append(s * scale + tab_ref[0, r0 - r + NA_KH - 1])
        probs = []
        for s in scores:
            p = jnp.exp(s - jnp.max(s, axis=-1, keepdims=True))
            probs.append((p.astype(BF16), jnp.sum(p, axis=-1, keepdims=True)))
        for r, r0, (p, l) in zip(rows, starts, probs):
            vw = v_ref[0, pl.ds(pl.multiple_of(r0 * GRID_W, GRID_W), span), :]
            o = jnp.dot(p, vw, preferred_element_type=F32) / l
            o_ref[pl.ds(pl.multiple_of(r * GRID_W, GRID_W), GRID_W), :] = o.astype(o_ref.dtype)
        return carry

    lax.fori_loop(0, grid_rows // NA_UNROLL, body, 0)


def _na_bias_table(rpb):
    qc = np.arange(GRID_W)[:, None]
    kc = np.arange(GRID_W)[None, :]
    wstart = np.clip(qc - NA_KW // 2, 0, GRID_W - NA_KW)
    valid = (kc >= wstart) & (kc < wstart + NA_KW)
    dc = np.clip(kc - qc, -(NA_KW - 1), NA_KW - 1) + NA_KW - 1
    dr = np.arange(NA_KH)[:, None] + np.arange(NA_KH)[None, :]
    tab = rpb[:, dr[:, :, None, None], dc[None, None, :, :]]
    tab = jnp.where(jnp.asarray(valid)[None, None, None], tab, MASK_NEG)
    return tab.transpose(0, 1, 3, 2, 4).reshape(NA_HEADS, NA_KH, GRID_W, NA_KH * GRID_W)


def neighborhood_attention(ph, rpb, *, batch, seq, slab0):
    grid_rows = seq // GRID_W
    assert grid_rows >= NA_KH and grid_rows % NA_UNROLL == 0
    tab = _na_bias_table(rpb.astype(F32))

    def slab(group):
        return pl.BlockSpec((1, seq, LANE), lambda b, h: (slab0 + group * NA_HEADS + h, b, 0))

    return pl.pallas_call(
        functools.partial(_na_kernel, grid_rows=grid_rows),
        grid=(batch, NA_HEADS),
        in_specs=[slab(0), slab(1), slab(2),
                  pl.BlockSpec((1, NA_KH, GRID_W, NA_KH * GRID_W), lambda b, h: (h, 0, 0, 0))],
        out_specs=pl.BlockSpec((seq, NA_DH), lambda b, h: (b, h)),
        out_shape=jax.ShapeDtypeStruct((batch * seq, NA_W), BF16),
        compiler_params=_params("arbitrary", "arbitrary"),
        name="neighborhood_attention",
    )(ph, ph, ph, tab)


def _hg_widths():
    widths, w = [], HG_CHUNK // 2
    while w >= HG_SUB:
        widths.append(w)
        w //= 2
    return widths


def _hg_kernel(lb_ref, zq_ref, zf_ref, zi_ref, *rest, rev, layer, n_chunks):
    if rev:
        of_ref, zg_ref, nw_ref, out_ref, st_ref, lvl_ref, dg_ref = rest
    else:
        out_ref, st_ref, lvl_ref, dg_ref = rest
    C = HG_CHUNK
    U = HG_SUB
    widths = _hg_widths()

    @pl.when(pl.program_id(2) == 0)
    def _():
        st_ref[...] = jnp.zeros_like(st_ref)
        s_i = lax.broadcasted_iota(I32, (C, C), 0)
        t_i = lax.broadcasted_iota(I32, (C, C), 1)
        earlier = (s_i > t_i) if rev else (s_i < t_i)
        split = jnp.where(earlier, s_i ^ t_i, 0)
        for l, w in enumerate(widths):
            lvl_ref[l] = jnp.where(split >= w, jnp.where(split < 2 * w, 1.0, 0.0), 0.0)
        near = jnp.where((s_i >= t_i) if rev else (s_i <= t_i), jnp.where((s_i ^ t_i) < U, 1.0, 0.0), 0.0)
        for tt in range(U):
            dg_ref[tt] = near * jnp.where((t_i & (U - 1)) == tt, 1.0, 0.0)

    lbp = lb_ref[...]
    pe = jnp.exp(lbp - jnp.max(lbp, axis=0, keepdims=True))
    share = pe / jnp.sum(pe, axis=0, keepdims=True)
    lb = jnp.sum(share[:layer + 1], axis=0, keepdims=True) - share[0:1]
    one_m_lb = 1.0 - lb

    ri = lax.broadcasted_iota(I32, (C, C), 0)
    ci_ = lax.broadcasted_iota(I32, (C, C), 1)
    tri = jnp.where((ri <= ci_) if rev else (ri >= ci_), 1.0, 0.0).astype(BF16)
    row = lax.broadcasted_iota(I32, (C, HG_DK), 0)

    def rows_of(step):
        cidx = (n_chunks - 1 - step) if rev else step
        return pl.ds(pl.multiple_of(cidx * C, C), C)

    def local_part(sl):
        q = _silu(zq_ref[0, sl, :].astype(F32))
        sig, nsig = _sigmoid_pair(zf_ref[0, sl, :].astype(F32))
        k = one_m_lb * nsig
        lf = jnp.log(lb + one_m_lb * sig)
        v = zi_ref[0, sl, :]

        hi = lf.astype(BF16)
        r1 = lf - hi.astype(F32)
        mid = r1.astype(BF16)
        lo = (r1 - mid.astype(F32)).astype(BF16)
        parts = jnp.dot(tri, jnp.concatenate([hi, mid, lo], axis=1), preferred_element_type=F32)
        b = parts[:, :LANE] + parts[:, LANE:2 * LANE] + parts[:, 2 * LANE:]

        att_t = jnp.zeros((C, C), F32)
        for l, w in enumerate(widths):
            b3 = b.reshape(C // (2 * w), 2 * w, HG_DK)
            edge = b3[:, w:w + 1, :] if rev else b3[:, w - 1:w, :]
            d = (b3 - edge).reshape(C, HG_DK)
            is_q = ((row & w) == 0) if rev else ((row & w) != 0)
            x = (jnp.where(is_q, q, k) * jnp.exp(jnp.where(is_q, d, -d))).astype(BF16)
            a = lax.dot_general(x, x, (((1,), (1,)), ((), ())), preferred_element_type=F32)
            att_t = att_t + a * lvl_ref[l]

        nu = C // U
        b8 = b.reshape(nu, U, HG_DK)
        q8 = q.reshape(nu, U, HG_DK)
        k8 = k.reshape(nu, U, HG_DK)
        for tt in range(U):
            e = jnp.exp(jnp.minimum(b8[:, tt:tt + 1, :] - b8, 0.0))
            a = jnp.sum(e * k8 * q8[:, tt:tt + 1, :], axis=-1, keepdims=True)
            att_t = att_t + a.reshape(C, 1) * dg_ref[tt]
        o = lax.dot_general(att_t.astype(BF16), v, (((0,), (0,)), ((), ())), preferred_element_type=F32)

        b_end = b[0:1, :] if rev else b[C - 1:C, :]
        kd = (k * jnp.exp(b_end - b)).astype(BF16)
        kv = lax.dot_general(v, kd, (((0,), (0,)), ((), ())), preferred_element_type=F32)
        return o, (q * jnp.exp(b)).astype(BF16), kv, jnp.exp(b_end)

    def carried_part(sl, local):
        o, q_dec, kv, end_decay = local
        state_t = st_ref[...]
        o = o + lax.dot_general(q_dec, state_t.astype(BF16), (((1,), (1,)), ((), ())),
                                preferred_element_type=F32)
        st_ref[...] = state_t * end_decay + kv
        if rev:
            o = o + of_ref[0, sl, :]
            o = o * lax.rsqrt(jnp.mean(o * o, axis=-1, keepdims=True) + NORM_EPS)
            res = o * nw_ref[...] * _silu(zg_ref[0, sl, :].astype(F32))
            out_ref[sl, :] = res.astype(out_ref.dtype)
        else:
            out_ref[0, sl, :] = o

    def group(it, carry):
        slices = [rows_of(it * HG_UNROLL + j) for j in range(HG_UNROLL)]
        locals_ = [local_part(sl) for sl in slices]
        for sl, local in zip(slices, locals_):
            carried_part(sl, local)
        return carry

    lax.fori_loop(0, n_chunks // HG_UNROLL, group, 0)


def _hg_pass(od_lb, ph, extra, *, rev, layer, batch, seq, rows):
    n_blk = seq // rows
    H = HG_HEADS

    def blk(step):
        return (n_blk - 1 - step) if rev else step

    def slab(group):
        return pl.BlockSpec((1, rows, LANE), lambda b, h, s: (group * H + h, b * n_blk + blk(s), 0))

    in_specs = [pl.BlockSpec((od_lb.shape[0], HG_DK), lambda b, h, s: (0, h)),
                slab(0), slab(2 if rev else 1), slab(3)]
    args = [od_lb, ph, ph, ph]
    if rev:
        o_f, norm_w = extra
        in_specs += [pl.BlockSpec((1, rows, LANE), lambda b, h, s: (h, b * n_blk + blk(s), 0)),
                     slab(4),
                     pl.BlockSpec((1, HG_DK), lambda b, h, s: (0, h))]
        args += [o_f, ph, norm_w.reshape(1, HG_W)]
        out_shape = jax.ShapeDtypeStruct((batch * seq, HG_W), BF16)
        out_spec = pl.BlockSpec((rows, HG_DK), lambda b, h, s: (b * n_blk + blk(s), h))
    else:
        out_shape = jax.ShapeDtypeStruct((H, batch * seq, LANE), F32)
        out_spec = pl.BlockSpec((1, rows, LANE), lambda b, h, s: (h, b * n_blk + blk(s), 0))
    return pl.pallas_call(
        functools.partial(_hg_kernel, rev=rev, layer=layer, n_chunks=rows // HG_CHUNK),
        grid=(batch, H, n_blk),
        in_specs=in_specs,
        out_specs=out_spec,
        out_shape=out_shape,
        scratch_shapes=[pltpu.VMEM((HG_DK, HG_DK), F32),
                        pltpu.VMEM((len(_hg_widths()), HG_CHUNK, HG_CHUNK), F32),
                        pltpu.VMEM((HG_SUB, HG_CHUNK, HG_CHUNK), F32)],
        compiler_params=_params("arbitrary", "arbitrary", "arbitrary"),
        name="hgrn2_bwd" if rev else "hgrn2_fwd",
    )(*args)


def hgrn2(ph, od_lb, norm_w, *, layer, batch, seq):
    rows = min(HG_ROWS, seq)
    od_lb = od_lb.astype(F32)
    o_f = _hg_pass(od_lb, ph, None, rev=False, layer=layer, batch=batch, seq=seq, rows=rows)
    return _hg_pass(od_lb, ph, (o_f, norm_w), rev=True, layer=layer, batch=batch, seq=seq, rows=rows)


def _router_kernel(wt_ref, bias_ref, tri_ref, x_ref, idx_ref, rank_ref, gw_ref, cnt_ref, run_ref):
    tm = x_ref.shape[0]
    G, P = N_GROUPS, E_PER_GROUP

    @pl.when(pl.program_id(0) == 0)
    def _():
        run_ref[...] = jnp.zeros_like(run_ref)

    logits = lax.dot_general(wt_ref[...], x_ref[...], (((1,), (1,)), ((), ())),
                             precision=lax.Precision.HIGHEST, preferred_element_type=F32)
    scores = _sigmoid_pair(logits)[0]
    sc3 = scores.reshape(G, P, tm)
    x3 = (scores + bias_ref[...]).reshape(G, P, tm)
    neg = -jnp.inf

    def max01(a):
        return jnp.max(jnp.max(a, axis=0, keepdims=True), axis=1, keepdims=True)

    def min01(a):
        return jnp.min(jnp.min(a, axis=0, keepdims=True), axis=1, keepdims=True)

    def sum01(a):
        return jnp.sum(jnp.sum(a, axis=0, keepdims=True), axis=1, keepdims=True)

    pidx = lax.broadcasted_iota(I32, (G, P, tm), 1)
    m1 = jnp.max(x3, axis=1, keepdims=True)
    i1 = jnp.min(jnp.where(x3 == m1, pidx, P), axis=1, keepdims=True)
    m2 = jnp.max(jnp.where(pidx == i1, neg, x3), axis=1, keepdims=True)
    gs = m1 + m2
    gidx = lax.broadcasted_iota(I32, (G, 1, tm), 0)
    before = jnp.zeros((G, 1, tm), I32)
    for g in range(G):
        other = gs[g:g + 1]
        ahead = jnp.where(other > gs, 1, jnp.where(other == gs, jnp.where(gidx > g, 1, 0), 0))
        before = before + ahead
    xm = jnp.where(before < TOPK_GROUPS, x3, neg)

    eidx = lax.broadcasted_iota(I32, (G, P, tm), 0) * P + pidx
    picks = []
    sel = jnp.zeros((G, P, tm), F32)
    for _ in range(TOP_K):
        m = max01(xm)
        ik = min01(jnp.where(xm == m, eidx, N_EXPERTS))
        hit = eidx == ik
        xm = jnp.where(hit, neg, xm)
        sel = jnp.where(hit, 1.0, sel)
        picks.append(ik)

    sel2 = sel.reshape(N_EXPERTS, tm)
    prefix = jnp.dot(sel2.astype(BF16), tri_ref[...], preferred_element_type=F32)
    rank3 = (run_ref[...] + prefix).reshape(G, P, tm)
    run_ref[...] = run_ref[...] + jnp.sum(sel2, axis=1, keepdims=True)
    cnt_ref[...] = run_ref[...].astype(I32)

    krow = lax.broadcasted_iota(I32, (TOP_K, tm), 0)
    idx_o = jnp.zeros((TOP_K, tm), I32)
    rank_o = jnp.zeros((TOP_K, tm), F32)
    gw_o = jnp.zeros((TOP_K, tm), F32)
    for kk, ik in enumerate(picks):
        hit = eidx == ik
        rk = sum01(jnp.where(hit, rank3, 0.0)).reshape(1, tm)
        sk = sum01(jnp.where(hit, sc3, 0.0)).reshape(1, tm)
        idx_o = jnp.where(krow == kk, ik.reshape(1, tm), idx_o)
        rank_o = jnp.where(krow == kk, rk, rank_o)
        gw_o = jnp.where(krow == kk, sk, gw_o)
    gw_o = gw_o / jnp.sum(gw_o, axis=0, keepdims=True) * ROUTED_SCALE
    idx_ref[...] = idx_o
    rank_ref[...] = rank_o.astype(I32)
    gw_ref[...] = gw_o


def router(x, w_router, r_bias):
    n, d = x.shape
    tm = min(ROUTER_TILE, n)
    tri = jnp.triu(jnp.ones((tm, tm), BF16), k=1)
    kspec = pl.BlockSpec((TOP_K, tm), lambda i: (0, i))
    return pl.pallas_call(
        _router_kernel,
        grid=(n // tm,),
        in_specs=[pl.BlockSpec((N_EXPERTS, d), lambda i: (0, 0)),
                  pl.BlockSpec((N_EXPERTS, 1), lambda i: (0, 0)),
                  pl.BlockSpec((tm, tm), lambda i: (0, 0)),
                  pl.BlockSpec((tm, d), lambda i: (i, 0))],
        out_specs=[kspec, kspec, kspec, pl.BlockSpec((N_EXPERTS, 1), lambda i: (0, 0))],
        out_shape=[jax.ShapeDtypeStruct((TOP_K, n), I32), jax.ShapeDtypeStruct((TOP_K, n), I32),
                   jax.ShapeDtypeStruct((TOP_K, n), F32), jax.ShapeDtypeStruct((N_EXPERTS, 1), I32)],
        scratch_shapes=[pltpu.VMEM((N_EXPERTS, 1), F32)],
        compiler_params=_params("arbitrary"),
        name="router",
    )(w_router.T, r_bias.astype(F32).reshape(N_EXPERTS, 1), tri, x)


def _dispatch_kernel(dest_ref, pend_ref, pcnt_ref, xp_ref, xg_ref, zero_ref, sem, *, n_tok):
    i = pl.program_id(0)
    tm = xp_ref.shape[0]

    def zero_block(start):
        return pltpu.make_async_copy(zero_ref, xg_ref.at[pl.ds(pl.multiple_of(start, MOE_BLOCK), MOE_BLOCK)], sem)

    @pl.when(i == 0)
    def _():
        zero_ref[...] = jnp.zeros_like(zero_ref)
        n_blocks = xg_ref.shape[0] // MOE_BLOCK
        n_used = pend_ref[N_EXPERTS - 1] // MOE_BLOCK

        def start(e, c):
            @pl.when(pcnt_ref[e] > 0)
            def _():
                zero_block(pend_ref[e] - MOE_BLOCK).start()
            return c

        def wait(e, c):
            @pl.when(pcnt_ref[e] > 0)
            def _():
                zero_block(pend_ref[e] - MOE_BLOCK).wait()
            return c

        def start_tail(blk, c):
            zero_block(blk * MOE_BLOCK).start()
            return c

        def wait_tail(blk, c):
            zero_block(blk * MOE_BLOCK).wait()
            return c

        lax.fori_loop(0, N_EXPERTS, start, 0)
        lax.fori_loop(n_used, n_blocks, start_tail, 0)
        lax.fori_loop(0, N_EXPERTS, wait, 0)
        lax.fori_loop(n_used, n_blocks, wait_tail, 0)

    def row_copy(t, k):
        d = dest_ref[k * n_tok + i * tm + t]
        return pltpu.make_async_copy(xp_ref.at[pl.ds(t, 1)], xg_ref.at[pl.ds(d, 1)], sem)

    def issue(t, c):
        for k in range(TOP_K):
            row_copy(t, k).start()
        return c

    lax.fori_loop(0, tm, issue, 0, unroll=8)
    for _ in range(TOP_K):
        pltpu.make_async_copy(xp_ref, xg_ref.at[pl.ds(0, tm)], sem).wait()


def dispatch(xp, dest, pends, pcounts, *, n_rows):
    n, half = xp.shape
    tm = min(DISPATCH_TILE, n)
    grid_spec = pltpu.PrefetchScalarGridSpec(
        num_scalar_prefetch=3,
        grid=(n // tm,),
        in_specs=[pl.BlockSpec((tm, half), lambda i, *_: (i, 0))],
        out_specs=pl.BlockSpec(memory_space=pl.ANY),
        scratch_shapes=[pltpu.VMEM((MOE_BLOCK, half), U32), pltpu.SemaphoreType.DMA],
    )
    return pl.pallas_call(
        functools.partial(_dispatch_kernel, n_tok=n),
        grid_spec=grid_spec,
        out_shape=jax.ShapeDtypeStruct((n_rows, half), U32),
        compiler_params=_params("arbitrary"),
        name="moe_dispatch",
    )(dest, pends, pcounts, xp)


def _expert_kernel(be_ref, nu_ref, x_ref, wg_ref, wu_ref, wd_ref, y_ref, wgu_s, wd_s):
    i = pl.program_id(0)
    used = i < nu_ref[0]
    prev = be_ref[jnp.maximum(i - 1, 0)]

    @pl.when(jnp.logical_and(used, jnp.logical_or(i == 0, be_ref[i] != prev)))
    def _():
        wgu_s[:, :D_EXPERT] = wg_ref[...].astype(BF16)
        wgu_s[:, D_EXPERT:] = wu_ref[...].astype(BF16)
        wd_s[...] = wd_ref[...].astype(BF16)

    @pl.when(used)
    def _():
        lo, hi = _unpack_rows(x_ref[...])
        x = jnp.concatenate([lo, hi], axis=1).astype(BF16)
        h = jnp.dot(x, wgu_s[...], preferred_element_type=F32)
        hid = (_silu(h[:, :D_EXPERT]) * h[:, D_EXPERT:]).astype(BF16)
        y_ref[...] = _pack_rows(jnp.dot(hid, wd_s[...], preferred_element_type=F32))

    @pl.when(jnp.logical_not(used))
    def _():
        y_ref[...] = jnp.zeros_like(y_ref)


def expert_blocks(xg, blk_e, n_used, w_gate, w_up, w_down, layer):
    n_rows, half = xg.shape
    d = 2 * half
    n_blocks = n_rows // MOE_BLOCK
    once = pl.Buffered(1)

    def xmap(i, be, nu):
        return (jnp.minimum(i, nu[0] - 1), 0)

    grid_spec = pltpu.PrefetchScalarGridSpec(
        num_scalar_prefetch=2,
        grid=(n_blocks,),
        in_specs=[pl.BlockSpec((MOE_BLOCK, half), xmap),
                  pl.BlockSpec((None, None, d, D_EXPERT), lambda i, be, nu: (layer, be[i], 0, 0),
                               pipeline_mode=once),
                  pl.BlockSpec((None, None, d, D_EXPERT), lambda i, be, nu: (layer, be[i], 0, 0),
                               pipeline_mode=once),
                  pl.BlockSpec((None, None, D_EXPERT, d), lambda i, be, nu: (layer, be[i], 0, 0),
                               pipeline_mode=once)],
        out_specs=pl.BlockSpec((MOE_BLOCK, half), lambda i, be, nu: (i, 0)),
        scratch_shapes=[pltpu.VMEM((d, 2 * D_EXPERT), BF16), pltpu.VMEM((D_EXPERT, d), BF16)],
    )
    return pl.pallas_call(
        _expert_kernel,
        grid_spec=grid_spec,
        out_shape=jax.ShapeDtypeStruct((n_rows, half), U32),
        compiler_params=_params("arbitrary"),
        name="expert_blocks",
    )(blk_e, n_used, xg, w_gate, w_up, w_down)


def _combine_kernel(dest_ref, x_ref, xb_ref, gw_ref, sg_ref, su_ref, sd_ref, g_ref, b_ref, yg_ref,
                    y_ref, yb_ref, yp_ref, buf_ref, sem, *, n_tok):
    i = pl.program_id(0)
    n_tiles = pl.num_programs(0)
    tm = x_ref.shape[0]
    slot = i % 2

    def gather(tile, to_slot):
        def issue(t, c):
            for k in range(TOP_K):
                d = dest_ref[k * n_tok + tile * tm + t]
                pltpu.make_async_copy(yg_ref.at[pl.ds(d, 1)], buf_ref.at[to_slot, k, pl.ds(t, 1)],
                                      sem.at[to_slot]).start()
            return c
        lax.fori_loop(0, tm, issue, 0, unroll=8)

    @pl.when(i == 0)
    def _():
        gather(0, 0)

    @pl.when(i + 1 < n_tiles)
    def _():
        gather(i + 1, 1 - slot)

    xb = xb_ref[...]
    hg = jnp.dot(xb, sg_ref[...], preferred_element_type=F32)
    hu = jnp.dot(xb, su_ref[...], preferred_element_type=F32)
    hid = (_silu(hg) * hu).astype(BF16)
    f = jnp.dot(hid, sd_ref[...], preferred_element_type=F32)

    for k in range(TOP_K):
        pltpu.make_async_copy(yg_ref.at[pl.ds(0, tm)], buf_ref.at[slot, k], sem.at[slot]).wait()
    gw = gw_ref[...]
    lo = jnp.zeros((tm, x_ref.shape[1] // 2), F32)
    hi = jnp.zeros((tm, x_ref.shape[1] // 2), F32)
    for k in range(TOP_K):
        l_k, h_k = _unpack_rows(buf_ref[slot, k])
        lo = lo + gw[:, k:k + 1] * l_k
        hi = hi + gw[:, k:k + 1] * h_k
    f = f + jnp.concatenate([lo, hi], axis=1)
    y = _layer_norm_rows(DEEPNORM_ALPHA * x_ref[...] + f, g_ref[...], b_ref[...])
    y_ref[...] = y
    yb_ref[...] = y.astype(BF16)
    yp_ref[...] = _pack_rows(y)


def moe_combine(x, xb, gw, dest, yg, s_gate, s_up, s_down, g, b):
    n, d = x.shape
    tm = min(COMBINE_TILE, n)
    once = pl.Buffered(1)
    row = pl.BlockSpec((tm, d), lambda i, *_: (i, 0))
    prow = pl.BlockSpec((tm, d // 2), lambda i, *_: (i, 0))
    vec = pl.BlockSpec((1, d), lambda i, *_: (0, 0), pipeline_mode=once)
    grid_spec = pltpu.PrefetchScalarGridSpec(
        num_scalar_prefetch=1,
        grid=(n // tm,),
        in_specs=[row, row, pl.BlockSpec((tm, TOP_K), lambda i, *_: (i, 0)),
                  pl.BlockSpec((d, D_EXPERT), lambda i, *_: (0, 0), pipeline_mode=once),
                  pl.BlockSpec((d, D_EXPERT), lambda i, *_: (0, 0), pipeline_mode=once),
                  pl.BlockSpec((D_EXPERT, d), lambda i, *_: (0, 0), pipeline_mode=once),
                  vec, vec, pl.BlockSpec(memory_space=pl.ANY)],
        out_specs=[row, row, prow],
        scratch_shapes=[pltpu.VMEM((2, TOP_K, tm, d // 2), U32), pltpu.SemaphoreType.DMA((2,))],
    )
    return pl.pallas_call(
        functools.partial(_combine_kernel, n_tok=n),
        grid_spec=grid_spec,
        out_shape=[jax.ShapeDtypeStruct((n, d), F32), jax.ShapeDtypeStruct((n, d), BF16),
                   jax.ShapeDtypeStruct((n, d // 2), U32)],
        compiler_params=_params("arbitrary"),
        name="moe_combine",
    )(dest, x, xb, gw, s_gate, s_up, s_down, g.reshape(1, d), b.reshape(1, d), yg)


def moe_ffn(x, xb, xp, layer, w_router, r_bias, w_gate, w_up, w_down, s_gate, s_up, s_down, ln_g, ln_b):
    n_tok = x.shape[0]
    idx, rank, gw, counts = router(x, w_router, r_bias)
    counts = counts.reshape(N_EXPERTS)
    pcounts = (counts + MOE_BLOCK - 1) // MOE_BLOCK * MOE_BLOCK
    pends = jnp.cumsum(pcounts).astype(I32)
    pstarts = pends - pcounts
    n_blocks = n_tok * TOP_K // MOE_BLOCK + N_EXPERTS
    blk_e = jnp.minimum(jnp.searchsorted(pends, jnp.arange(n_blocks, dtype=I32) * MOE_BLOCK, side='right'),
                        N_EXPERTS - 1).astype(I32)
    n_used = (pends[-1:] // MOE_BLOCK).astype(I32)
    dest = (pstarts[idx] + rank).astype(I32).reshape(TOP_K * n_tok)

    xg = dispatch(xp, dest, pends, pcounts.astype(I32), n_rows=n_blocks * MOE_BLOCK)
    yg = expert_blocks(xg, blk_e, n_used, w_gate, w_up, w_down, layer)
    return moe_combine(x, xb, gw.T, dest, yg, s_gate, s_up, s_down, ln_g, ln_b)


def kernel(x, ln_g, ln_b, ev_w_in, ev_decay_exp, ev_gn_w, ev_rpb, ev_w_out, od_w_in, od_lb, od_norm_w, od_w_out, moe_w_router, moe_bias, moe_w_gate, moe_w_up, moe_w_down, sh_w_gate, sh_w_up, sh_w_down):
    batch, seq, d = x.shape
    n = batch * seq
    xf = x.reshape(n, d).astype(F32)
    xb = xf.astype(BF16)
    for layer in range(DEPTH):
        j = layer // 2
        if layer % 2 == 0:
            ph = matmul(xb, ev_w_in, j, out_dtype=BF16, heads=True)
            o_ret = retention(ph, ev_decay_exp[j], ev_gn_w[j].astype(F32), batch=batch, seq=seq)
            o_na = neighborhood_attention(ph, ev_rpb[j], batch=batch, seq=seq, slab0=4 * RET_W // LANE)
            mixed = jnp.concatenate([o_ret, o_na], axis=1)
            w_out = ev_w_out
        else:
            ph = matmul(xb, od_w_in, j, out_dtype=BF16, heads=True)
            mixed = hgrn2(ph, od_lb, od_norm_w[j].astype(F32), layer=layer, batch=batch, seq=seq)
            w_out = od_w_out
        m = matmul(mixed, w_out, j, out_dtype=F32, heads=False)
        xf, xb, xp = ln_residual(xf, m, ln_g[layer, 0].astype(F32), ln_b[layer, 0].astype(F32))
        xf, xb, xp = moe_ffn(xf, xb, xp, layer, moe_w_router[layer].astype(F32), moe_bias[layer],
                             moe_w_gate, moe_w_up, moe_w_down, sh_w_gate[layer].astype(BF16),
                             sh_w_up[layer].astype(BF16), sh_w_down[layer].astype(BF16),
                             ln_g[layer, 1].astype(F32), ln_b[layer, 1].astype(F32))
    return xf.reshape(batch, seq, d)
```

```python
import functools

import numpy as np
import jax
import jax.numpy as jnp
from jax import lax
from jax.experimental import pallas as pl
from jax.experimental.pallas import tpu as pltpu

F32 = jnp.float32
BF16 = jnp.bfloat16
U32 = jnp.uint32
I32 = jnp.int32

LANE = 128
SUBLANE = 8
VMEM_LIMIT = 56 * 1024 * 1024

DEPTH = 2
GRID_W = 64
RET_HEADS = 8
RET_DK = 256
RET_W = RET_HEADS * RET_DK
RET_CHUNK = 128
ROPE_BASE = 10000.0
NA_HEADS = 16
NA_DH = 128
NA_W = NA_HEADS * NA_DH
NA_KH = 8
NA_KW = 16
NA_UNROLL = 8
HG_HEADS = 32
HG_DK = 128
HG_W = HG_HEADS * HG_DK
HG_CHUNK = 128
HG_SUB = SUBLANE
HG_ROWS = 1024
HG_UNROLL = 4
N_EXPERTS = 64
TOP_K = 8
N_GROUPS = 8
TOPK_GROUPS = 4
E_PER_GROUP = N_EXPERTS // N_GROUPS
D_EXPERT = 384
ROUTED_SCALE = 2.5
MOE_BLOCK = 256
ROUTER_TILE = 512
DISPATCH_TILE = 256
COMBINE_TILE = 128
DEEPNORM_ALPHA = (2.0 * DEPTH) ** 0.25
LN_EPS = 1e-5
NORM_EPS = 1e-6
MASK_NEG = -1e30


def _params(*sem):
    return pltpu.CompilerParams(dimension_semantics=sem, vmem_limit_bytes=VMEM_LIMIT)


def _sigmoid_pair(z):
    e = jnp.exp(-jnp.abs(z))
    r = 1.0 / (1.0 + e)
    er = e * r
    pos = z >= 0
    return jnp.where(pos, r, er), jnp.where(pos, er, r)


def _silu(z):
    return z * _sigmoid_pair(z)[0]


def _pack_rows(y):
    half = y.shape[1] // 2
    lo = lax.bitcast_convert_type(y[:, :half].astype(BF16).astype(F32), U32) >> 16
    hi = lax.bitcast_convert_type(y[:, half:].astype(BF16).astype(F32), U32) & jnp.uint32(0xFFFF0000)
    return hi | lo


def _unpack_rows(u):
    lo = lax.bitcast_convert_type(u << 16, F32)
    hi = lax.bitcast_convert_type(u & jnp.uint32(0xFFFF0000), F32)
    return lo, hi


def _mm_kernel(a_ref, w_ref, o_ref, wb_ref, *, heads):
    @pl.when(pl.program_id(1) == 0)
    def _():
        wb_ref[...] = w_ref[...].astype(BF16)

    acc = jnp.dot(a_ref[...], wb_ref[...], preferred_element_type=F32)
    if heads:
        for j in range(o_ref.shape[0]):
            o_ref[j] = acc[:, j * LANE:(j + 1) * LANE].astype(o_ref.dtype)
    else:
        o_ref[...] = acc.astype(o_ref.dtype)


def matmul(a, w, layer, *, out_dtype, heads, tm=1024, tn=512):
    n, k = a.shape
    m = w.shape[2]
    tm, tn = min(tm, n), min(tn, m)
    if heads:
        out_shape = jax.ShapeDtypeStruct((m // LANE, n, LANE), out_dtype)
        out_spec = pl.BlockSpec((tn // LANE, tm, LANE), lambda j, i: (j, i, 0))
    else:
        out_shape = jax.ShapeDtypeStruct((n, m), out_dtype)
        out_spec = pl.BlockSpec((tm, tn), lambda j, i: (i, j))
    return pl.pallas_call(
        functools.partial(_mm_kernel, heads=heads),
        grid=(m // tn, n // tm),
        in_specs=[pl.BlockSpec((tm, k), lambda j, i: (i, 0)),
                  pl.BlockSpec((None, k, tn), lambda j, i: (layer, 0, j))],
        out_specs=out_spec,
        out_shape=out_shape,
        scratch_shapes=[pltpu.VMEM((k, tn), BF16)],
        compiler_params=_params("arbitrary", "arbitrary"),
        name="proj_matmul",
    )(a, w)


def _layer_norm_rows(z, g, b):
    mu = jnp.mean(z, axis=-1, keepdims=True)
    zc = z - mu
    var = jnp.mean(zc * zc, axis=-1, keepdims=True)
    return zc * lax.rsqrt(var + LN_EPS) * g + b


def _ln_kernel(x_ref, m_ref, g_ref, b_ref, y_ref, yb_ref, yp_ref):
    z = DEEPNORM_ALPHA * x_ref[...] + m_ref[...].astype(F32)
    y = _layer_norm_rows(z, g_ref[...], b_ref[...])
    y_ref[...] = y
    yb_ref[...] = y.astype(BF16)
    yp_ref[...] = _pack_rows(y)


def ln_residual(x, m, g, b, *, tm=256):
    n, d = x.shape
    tm = min(tm, n)
    row = pl.BlockSpec((tm, d), lambda i: (i, 0))
    prow = pl.BlockSpec((tm, d // 2), lambda i: (i, 0))
    vec = pl.BlockSpec((1, d), lambda i: (0, 0))
    return pl.pallas_call(
        _ln_kernel,
        grid=(n // tm,),
        in_specs=[row, row, vec, vec],
        out_specs=[row, row, prow],
        out_shape=[jax.ShapeDtypeStruct((n, d), F32), jax.ShapeDtypeStruct((n, d), BF16),
                   jax.ShapeDtypeStruct((n, d // 2), U32)],
        compiler_params=_params("arbitrary"),
        name="ln_residual",
    )(x, m, g.reshape(1, d), b.reshape(1, d))


def _ret_kernel(lg_ref, q_ref, k_ref, v_ref, cos_ref, sin_ref, *rest, rev, n_chunks):
    if rev:
        oa_ref, g_ref, gn_ref, out_ref, s_ref = rest
    else:
        out_ref, s_ref = rest
    C = RET_CHUNK
    h = pl.program_id(1)

    @pl.when(pl.program_id(2) == 0)
    def _():
        s_ref[...] = jnp.zeros_like(s_ref)

    lg = lg_ref[1 if rev else 0, h]
    row = lax.broadcasted_iota(I32, (C, RET_DK), 0).astype(F32)
    if rev:
        q_dec = jnp.exp(lg * (C - row))
        k_dec = jnp.exp(lg * row)
    else:
        q_dec = jnp.exp(lg * (row + 1.0))
        k_dec = jnp.exp(lg * (C - 1.0 - row))
        dist = (lax.broadcasted_iota(I32, (C, C), 0)
                - lax.broadcasted_iota(I32, (C, C), 1)).astype(F32)
        decay = jnp.where(dist >= 0,
                          jnp.exp(lg_ref[0, h] * jnp.maximum(dist, 0.0)),
                          jnp.exp(lg_ref[1, h] * jnp.maximum(-dist, 0.0)))
    chunk_decay = jnp.exp(jnp.full((1, 1), C, F32) * lg)

    def rot(ref, sl, cos, sin):
        t1 = ref[0, sl, :].astype(F32)
        t2 = ref[1, sl, :].astype(F32)
        return jnp.concatenate([t1 * cos - t2 * sin, t1 * sin + t2 * cos], axis=1)

    order = range(n_chunks - 1, -1, -1) if rev else range(n_chunks)
    for ci in order:
        sl = slice(ci * C, (ci + 1) * C)
        cos, sin = cos_ref[sl, :], sin_ref[sl, :]
        qr = rot(q_ref, sl, cos, sin)
        kr = rot(k_ref, sl, cos, sin) * (RET_DK ** -0.5)
        v = jnp.concatenate([v_ref[0, sl, :], v_ref[1, sl, :]], axis=1)
        state = s_ref[...]
        o = jnp.dot((qr * q_dec).astype(BF16), state.astype(BF16), preferred_element_type=F32)
        if rev:
            o = o + jnp.concatenate([oa_ref[0, sl, :], oa_ref[1, sl, :]], axis=1)
            mu = jnp.mean(o, axis=-1, keepdims=True)
            oc = o - mu
            var = jnp.mean(oc * oc, axis=-1, keepdims=True)
            gate = jnp.concatenate([g_ref[0, sl, :], g_ref[1, sl, :]], axis=1).astype(F32)
            res = oc * lax.rsqrt(var + NORM_EPS) * gn_ref[...] * _silu(gate)
            out_ref[sl, :] = res.astype(out_ref.dtype)
        else:
            scores = lax.dot_general(qr.astype(BF16), kr.astype(BF16), (((1,), (1,)), ((), ())),
                                     preferred_element_type=F32) * decay
            o = o + jnp.dot(scores.astype(BF16), v, preferred_element_type=F32)
            out_ref[0, sl, :] = o[:, :LANE]
            out_ref[1, sl, :] = o[:, LANE:]
        kv = lax.dot_general((kr * k_dec).astype(BF16), v, (((0,), (0,)), ((), ())),
                             preferred_element_type=F32)
        s_ref[...] = chunk_decay * state + kv


def _ret_pass(lg, ph, cos, sin, extra, *, rev, batch, seq, rows):
    n_blk = seq // rows
    n_chunks = rows // RET_CHUNK
    H = RET_HEADS

    def blk(step):
        return (n_blk - 1 - step) if rev else step

    def slab(group):
        return pl.BlockSpec((2, rows, LANE), lambda b, h, s: (group * H + h, b * n_blk + blk(s), 0))

    tab = pl.BlockSpec((rows, LANE), lambda b, h, s: (blk(s), 0))
    in_specs = [pl.BlockSpec(memory_space=pltpu.SMEM), slab(0), slab(1), slab(2), tab, tab]
    args = [lg, ph, ph, ph, cos, sin]
    if rev:
        oa, gn_w = extra
        in_specs += [pl.BlockSpec((2, rows, LANE), lambda b, h, s: (h, b * n_blk + blk(s), 0)),
                     slab(3),
                     pl.BlockSpec((1, RET_DK), lambda b, h, s: (0, h))]
        args += [oa, ph, gn_w.reshape(1, RET_W)]
        out_shape = jax.ShapeDtypeStruct((batch * seq, RET_W), BF16)
        out_spec = pl.BlockSpec((rows, RET_DK), lambda b, h, s: (b * n_blk + blk(s), h))
    else:
        out_shape = jax.ShapeDtypeStruct((2 * H, batch * seq, LANE), F32)
        out_spec = pl.BlockSpec((2, rows, LANE), lambda b, h, s: (h, b * n_blk + blk(s), 0))
    return pl.pallas_call(
        functools.partial(_ret_kernel, rev=rev, n_chunks=n_chunks),
        grid=(batch, H, n_blk),
        in_specs=in_specs,
        out_specs=out_spec,
        out_shape=out_shape,
        scratch_shapes=[pltpu.VMEM((RET_DK, RET_DK), F32)],
        compiler_params=_params("arbitrary", "arbitrary", "arbitrary"),
        name="retention_bwd" if rev else "retention_fwd",
    )(*args)


def retention(ph, decay_exp, gn_w, *, batch, seq, rows=512):
    rows = min(rows, seq)
    log_g = jnp.log1p(-jnp.exp2(-decay_exp.astype(F32)))
    inv = ROPE_BASE ** (-jnp.linspace(0.0, 1.0, RET_DK // 2, dtype=F32))
    ang = jnp.arange(seq, dtype=F32)[:, None] * inv[None, :]
    cos, sin = jnp.cos(ang), jnp.sin(ang)
    o_a = _ret_pass(log_g, ph, cos, sin, None, rev=False, batch=batch, seq=seq, rows=rows)
    return _ret_pass(log_g, ph, cos, sin, (o_a, gn_w), rev=True, batch=batch, seq=seq, rows=rows)


def _na_kernel(q_ref, k_ref, v_ref, tab_ref, o_ref, *, grid_rows):
    span = NA_KH * GRID_W
    scale = NA_DH ** -0.5

    def body(it, carry):
        rows = [it * NA_UNROLL + j for j in range(NA_UNROLL)]
        starts = [jnp.clip(r - NA_KH // 2, 0, grid_rows - NA_KH) for r in rows]
        scores = []
        for r, r0 in zip(rows, starts):
            q = q_ref[0, pl.ds(pl.multiple_of(r * GRID_W, GRID_W), GRID_W), :]
            kw = k_ref[0, pl.ds(pl.multiple_of(r0 * GRID_W, GRID_W), span), :]
            s = lax.dot_general(q, kw, (((1,), (1,)), ((), ())), preferred_element_type=F32)
            scores.append(s * scale + tab_ref[0, r0 - r + NA_KH - 1])
        probs = []
        for s in scores:
            p = jnp.exp(s - jnp.max(s, axis=-1, keepdims=True))
            probs.append((p.astype(BF16), jnp.sum(p, axis=-1, keepdims=True)))
        for r, r0, (p, l) in zip(rows, starts, probs):
            vw = v_ref[0, pl.ds(pl.multiple_of(r0 * GRID_W, GRID_W), span), :]
            o = jnp.dot(p, vw, preferred_element_type=F32) / l
            o_ref[pl.ds(pl.multiple_of(r * GRID_W, GRID_W), GRID_W), :] = o.astype(o_ref.dtype)
        return carry

    lax.fori_loop(0, grid_rows // NA_UNROLL, body, 0)


def _na_bias_table(rpb):
    qc = np.arange(GRID_W)[:, None]
    kc = np.arange(GRID_W)[None, :]
    wstart = np.clip(qc - NA_KW // 2, 0, GRID_W - NA_KW)
    valid = (kc >= wstart) & (kc < wstart + NA_KW)
    dc = np.clip(kc - qc, -(NA_KW - 1), NA_KW - 1) + NA_KW - 1
    onehot = ((dc[None] == np.arange(2 * NA_KW - 1)[:, None, None]) & valid[None]).astype(np.float32)
    by_col = jnp.einsum('hrj,jck->hrck', rpb, jnp.asarray(onehot), precision=lax.Precision.HIGHEST)
    by_col = by_col + jnp.asarray(np.where(valid, 0.0, MASK_NEG).astype(np.float32))[None, None]
    tab = jnp.stack([by_col[:, off:off + NA_KH] for off in range(NA_KH)], axis=1)
    return tab.transpose(0, 1, 3, 2, 4).reshape(NA_HEADS, NA_KH, GRID_W, NA_KH * GRID_W)


def neighborhood_attention(ph, rpb, *, batch, seq, slab0):
    grid_rows = seq // GRID_W
    assert grid_rows >= NA_KH and grid_rows % NA_UNROLL == 0
    tab = _na_bias_table(rpb.astype(F32))

    def slab(group):
        return pl.BlockSpec((1, seq, LANE), lambda b, h: (slab0 + group * NA_HEADS + h, b, 0))

    return pl.pallas_call(
        functools.partial(_na_kernel, grid_rows=grid_rows),
        grid=(batch, NA_HEADS),
        in_specs=[slab(0), slab(1), slab(2),
                  pl.BlockSpec((1, NA_KH, GRID_W, NA_KH * GRID_W), lambda b, h: (h, 0, 0, 0))],
        out_specs=pl.BlockSpec((seq, NA_DH), lambda b, h: (b, h)),
        out_shape=jax.ShapeDtypeStruct((batch * seq, NA_W), BF16),
        compiler_params=_params("arbitrary", "arbitrary"),
        name="neighborhood_attention",
    )(ph, ph, ph, tab)


def _hg_widths():
    widths, w = [], HG_CHUNK // 2
    while w >= HG_SUB:
        widths.append(w)
        w //= 2
    return widths


def _hg_kernel(lb_ref, zq_ref, zf_ref, zi_ref, *rest, rev, layer, n_chunks):
    if rev:
        of_ref, zg_ref, nw_ref, out_ref, st_ref, lvl_ref, dg_ref = rest
    else:
        out_ref, st_ref, lvl_ref, dg_ref = rest
    C = HG_CHUNK
    U = HG_SUB
    widths = _hg_widths()

    @pl.when(pl.program_id(2) == 0)
    def _():
        st_ref[...] = jnp.zeros_like(st_ref)
        s_i = lax.broadcasted_iota(I32, (C, C), 0)
        t_i = lax.broadcasted_iota(I32, (C, C), 1)
        earlier = (s_i > t_i) if rev else (s_i < t_i)
        split = jnp.where(earlier, s_i ^ t_i, 0)
        for l, w in enumerate(widths):
            lvl_ref[l] = jnp.where(split >= w, jnp.where(split < 2 * w, 1.0, 0.0), 0.0)
        near = jnp.where((s_i >= t_i) if rev else (s_i <= t_i), jnp.where((s_i ^ t_i) < U, 1.0, 0.0), 0.0)
        for tt in range(U):
            dg_ref[tt] = near * jnp.where((t_i & (U - 1)) == tt, 1.0, 0.0)

    lbp = lb_ref[...]
    pe = jnp.exp(lbp - jnp.max(lbp, axis=0, keepdims=True))
    share = pe / jnp.sum(pe, axis=0, keepdims=True)
    lb = jnp.sum(share[:layer + 1], axis=0, keepdims=True) - share[0:1]
    one_m_lb = 1.0 - lb

    ri = lax.broadcasted_iota(I32, (C, C), 0)
    ci_ = lax.broadcasted_iota(I32, (C, C), 1)
    tri = jnp.where((ri <= ci_) if rev else (ri >= ci_), 1.0, 0.0).astype(BF16)
    row = lax.broadcasted_iota(I32, (C, HG_DK), 0)

    def rows_of(step):
        cidx = (n_chunks - 1 - step) if rev else step
        return pl.ds(pl.multiple_of(cidx * C, C), C)

    def local_part(sl):
        q = _silu(zq_ref[0, sl, :].astype(F32))
        sig, nsig = _sigmoid_pair(zf_ref[0, sl, :].astype(F32))
        k = one_m_lb * nsig
        lf = jnp.log(lb + one_m_lb * sig)
        v = zi_ref[0, sl, :]

        hi = lf.astype(BF16)
        r1 = lf - hi.astype(F32)
        mid = r1.astype(BF16)
        lo = (r1 - mid.astype(F32)).astype(BF16)
        parts = jnp.dot(tri, jnp.concatenate([hi, mid, lo], axis=1), preferred_element_type=F32)
        b = parts[:, :LANE] + parts[:, LANE:2 * LANE] + parts[:, 2 * LANE:]

        att_t = jnp.zeros((C, C), F32)
        for l, w in enumerate(widths):
            b3 = b.reshape(C // (2 * w), 2 * w, HG_DK)
            edge = b3[:, w:w + 1, :] if rev else b3[:, w - 1:w, :]
            d = (b3 - edge).reshape(C, HG_DK)
            is_q = ((row & w) == 0) if rev else ((row & w) != 0)
            x = (jnp.where(is_q, q, k) * jnp.exp(jnp.where(is_q, d, -d))).astype(BF16)
            a = lax.dot_general(x, x, (((1,), (1,)), ((), ())), preferred_element_type=F32)
            att_t = att_t + a * lvl_ref[l]

        nu = C // U
        b8 = b.reshape(nu, U, HG_DK)
        q8 = q.reshape(nu, U, HG_DK)
        k8 = k.reshape(nu, U, HG_DK)
        for tt in range(U):
            e = jnp.exp(jnp.minimum(b8[:, tt:tt + 1, :] - b8, 0.0))
            a = jnp.sum(e * k8 * q8[:, tt:tt + 1, :], axis=-1, keepdims=True)
            att_t = att_t + a.reshape(C, 1) * dg_ref[tt]
        o = lax.dot_general(att_t.astype(BF16), v, (((0,), (0,)), ((), ())), preferred_element_type=F32)

        b_end = b[0:1, :] if rev else b[C - 1:C, :]
        kd = (k * jnp.exp(b_end - b)).astype(BF16)
        kv = lax.dot_general(v, kd, (((0,), (0,)), ((), ())), preferred_element_type=F32)
        return o, (q * jnp.exp(b)).astype(BF16), kv, jnp.exp(b_end)

    def carried_part(sl, local):
        o, q_dec, kv, end_decay = local
        state_t = st_ref[...]
        o = o + lax.dot_general(q_dec, state_t.astype(BF16), (((1,), (1,)), ((), ())),
                                preferred_element_type=F32)
        st_ref[...] = state_t * end_decay + kv
        if rev:
            o = o + of_ref[0, sl, :]
            o = o * lax.rsqrt(jnp.mean(o * o, axis=-1, keepdims=True) + NORM_EPS)
            res = o * nw_ref[...] * _silu(zg_ref[0, sl, :].astype(F32))
            out_ref[sl, :] = res.astype(out_ref.dtype)
        else:
            out_ref[0, sl, :] = o

    def group(it, carry):
        slices = [rows_of(it * HG_UNROLL + j) for j in range(HG_UNROLL)]
        locals_ = [local_part(sl) for sl in slices]
        for sl, local in zip(slices, locals_):
            carried_part(sl, local)
        return carry

    lax.fori_loop(0, n_chunks // HG_UNROLL, group, 0)


def _hg_pass(od_lb, ph, extra, *, rev, layer, batch, seq, rows):
    n_blk = seq // rows
    H = HG_HEADS

    def blk(step):
        return (n_blk - 1 - step) if rev else step

    def slab(group):
        return pl.BlockSpec((1, rows, LANE), lambda b, h, s: (group * H + h, b * n_blk + blk(s), 0))

    in_specs = [pl.BlockSpec((od_lb.shape[0], HG_DK), lambda b, h, s: (0, h)),
                slab(0), slab(2 if rev else 1), slab(3)]
    args = [od_lb, ph, ph, ph]
    if rev:
        o_f, norm_w = extra
        in_specs += [pl.BlockSpec((1, rows, LANE), lambda b, h, s: (h, b * n_blk + blk(s), 0)),
                     slab(4),
                     pl.BlockSpec((1, HG_DK), lambda b, h, s: (0, h))]
        args += [o_f, ph, norm_w.reshape(1, HG_W)]
        out_shape = jax.ShapeDtypeStruct((batch * seq, HG_W), BF16)
        out_spec = pl.BlockSpec((rows, HG_DK), lambda b, h, s: (b * n_blk + blk(s), h))
    else:
        out_shape = jax.ShapeDtypeStruct((H, batch * seq, LANE), F32)
        out_spec = pl.BlockSpec((1, rows, LANE), lambda b, h, s: (h, b * n_blk + blk(s), 0))
    return pl.pallas_call(
        functools.partial(_hg_kernel, rev=rev, layer=layer, n_chunks=rows // HG_CHUNK),
        grid=(batch, H, n_blk),
        in_specs=in_specs,
        out_specs=out_spec,
        out_shape=out_shape,
        scratch_shapes=[pltpu.VMEM((HG_DK, HG_DK), F32),
                        pltpu.VMEM((len(_hg_widths()), HG_CHUNK, HG_CHUNK), F32),
                        pltpu.VMEM((HG_SUB, HG_CHUNK, HG_CHUNK), F32)],
        compiler_params=_params("arbitrary", "arbitrary", "arbitrary"),
        name="hgrn2_bwd" if rev else "hgrn2_fwd",
    )(*args)


def hgrn2(ph, od_lb, norm_w, *, layer, batch, seq):
    rows = min(HG_ROWS, seq)
    od_lb = od_lb.astype(F32)
    o_f = _hg_pass(od_lb, ph, None, rev=False, layer=layer, batch=batch, seq=seq, rows=rows)
    return _hg_pass(od_lb, ph, (o_f, norm_w), rev=True, layer=layer, batch=batch, seq=seq, rows=rows)


def _router_kernel(wt_ref, bias_ref, tri_ref, x_ref, idx_ref, rank_ref, gw_ref, cnt_ref, run_ref):
    tm = x_ref.shape[0]
    G, P = N_GROUPS, E_PER_GROUP

    @pl.when(pl.program_id(0) == 0)
    def _():
        run_ref[...] = jnp.zeros_like(run_ref)

    logits = lax.dot_general(wt_ref[...], x_ref[...], (((1,), (1,)), ((), ())),
                             precision=lax.Precision.HIGHEST, preferred_element_type=F32)
    scores = _sigmoid_pair(logits)[0]
    sc3 = scores.reshape(G, P, tm)
    x3 = (scores + bias_ref[...]).reshape(G, P, tm)
    neg = -jnp.inf

    def max01(a):
        return jnp.max(jnp.max(a, axis=0, keepdims=True), axis=1, keepdims=True)

    def min01(a):
        return jnp.min(jnp.min(a, axis=0, keepdims=True), axis=1, keepdims=True)

    def sum01(a):
        return jnp.sum(jnp.sum(a, axis=0, keepdims=True), axis=1, keepdims=True)

    pidx = lax.broadcasted_iota(I32, (G, P, tm), 1)
    m1 = jnp.max(x3, axis=1, keepdims=True)
    i1 = jnp.min(jnp.where(x3 == m1, pidx, P), axis=1, keepdims=True)
    m2 = jnp.max(jnp.where(pidx == i1, neg, x3), axis=1, keepdims=True)
    gs = m1 + m2
    gidx = lax.broadcasted_iota(I32, (G, 1, tm), 0)
    before = jnp.zeros((G, 1, tm), I32)
    for g in range(G):
        other = gs[g:g + 1]
        ahead = jnp.where(other > gs, 1, jnp.where(other == gs, jnp.where(gidx > g, 1, 0), 0))
        before = before + ahead
    xm = jnp.where(before < TOPK_GROUPS, x3, neg)

    eidx = lax.broadcasted_iota(I32, (G, P, tm), 0) * P + pidx
    picks = []
    sel = jnp.zeros((G, P, tm), F32)
    for _ in range(TOP_K):
        m = max01(xm)
        ik = min01(jnp.where(xm == m, eidx, N_EXPERTS))
        hit = eidx == ik
        xm = jnp.where(hit, neg, xm)
        sel = jnp.where(hit, 1.0, sel)
        picks.append(ik)

    sel2 = sel.reshape(N_EXPERTS, tm)
    prefix = jnp.dot(sel2.astype(BF16), tri_ref[...], preferred_element_type=F32)
    rank3 = (run_ref[...] + prefix).reshape(G, P, tm)
    run_ref[...] = run_ref[...] + jnp.sum(sel2, axis=1, keepdims=True)
    cnt_ref[...] = run_ref[...].astype(I32)

    krow = lax.broadcasted_iota(I32, (TOP_K, tm), 0)
    idx_o = jnp.zeros((TOP_K, tm), I32)
    rank_o = jnp.zeros((TOP_K, tm), F32)
    gw_o = jnp.zeros((TOP_K, tm), F32)
    for kk, ik in enumerate(picks):
        hit = eidx == ik
        rk = sum01(jnp.where(hit, rank3, 0.0)).reshape(1, tm)
        sk = sum01(jnp.where(hit, sc3, 0.0)).reshape(1, tm)
        idx_o = jnp.where(krow == kk, ik.reshape(1, tm), idx_o)
        rank_o = jnp.where(krow == kk, rk, rank_o)
        gw_o = jnp.where(krow == kk, sk, gw_o)
    gw_o = gw_o / jnp.sum(gw_o, axis=0, keepdims=True) * ROUTED_SCALE
    idx_ref[...] = idx_o
    rank_ref[...] = rank_o.astype(I32)
    gw_ref[...] = gw_o


def router(x, w_router, r_bias):
    n, d = x.shape
    tm = min(ROUTER_TILE, n)
    tri = jnp.triu(jnp.ones((tm, tm), BF16), k=1)
    kspec = pl.BlockSpec((TOP_K, tm), lambda i: (0, i))
    return pl.pallas_call(
        _router_kernel,
        grid=(n // tm,),
        in_specs=[pl.BlockSpec((N_EXPERTS, d), lambda i: (0, 0)),
                  pl.BlockSpec((N_EXPERTS, 1), lambda i: (0, 0)),
                  pl.BlockSpec((tm, tm), lambda i: (0, 0)),
                  pl.BlockSpec((tm, d), lambda i: (i, 0))],
        out_specs=[kspec, kspec, kspec, pl.BlockSpec((N_EXPERTS, 1), lambda i: (0, 0))],
        out_shape=[jax.ShapeDtypeStruct((TOP_K, n), I32), jax.ShapeDtypeStruct((TOP_K, n), I32),
                   jax.ShapeDtypeStruct((TOP_K, n), F32), jax.ShapeDtypeStruct((N_EXPERTS, 1), I32)],
        scratch_shapes=[pltpu.VMEM((N_EXPERTS, 1), F32)],
        compiler_params=_params("arbitrary"),
        name="router",
    )(w_router.T, r_bias.astype(F32).reshape(N_EXPERTS, 1), tri, x)


def _dispatch_kernel(dest_ref, pend_ref, pcnt_ref, xp_ref, xg_ref, zero_ref, sem, *, n_tok):
    i = pl.program_id(0)
    tm = xp_ref.shape[0]

    def zero_block(start):
        return pltpu.make_async_copy(zero_ref, xg_ref.at[pl.ds(pl.multiple_of(start, MOE_BLOCK), MOE_BLOCK)], sem)

    @pl.when(i == 0)
    def _():
        zero_ref[...] = jnp.zeros_like(zero_ref)
        n_blocks = xg_ref.shape[0] // MOE_BLOCK
        n_used = pend_ref[N_EXPERTS - 1] // MOE_BLOCK

        def start(e, c):
            @pl.when(pcnt_ref[e] > 0)
            def _():
                zero_block(pend_ref[e] - MOE_BLOCK).start()
            return c

        def wait(e, c):
            @pl.when(pcnt_ref[e] > 0)
            def _():
                zero_block(pend_ref[e] - MOE_BLOCK).wait()
            return c

        def start_tail(blk, c):
            zero_block(blk * MOE_BLOCK).start()
            return c

        def wait_tail(blk, c):
            zero_block(blk * MOE_BLOCK).wait()
            return c

        lax.fori_loop(0, N_EXPERTS, start, 0)
        lax.fori_loop(n_used, n_blocks, start_tail, 0)
        lax.fori_loop(0, N_EXPERTS, wait, 0)
        lax.fori_loop(n_used, n_blocks, wait_tail, 0)

    def row_copy(t, k):
        d = dest_ref[k * n_tok + i * tm + t]
        return pltpu.make_async_copy(xp_ref.at[pl.ds(t, 1)], xg_ref.at[pl.ds(d, 1)], sem)

    def issue(t, c):
        for k in range(TOP_K):
            row_copy(t, k).start()
        return c

    lax.fori_loop(0, tm, issue, 0, unroll=8)
    for _ in range(TOP_K):
        pltpu.make_async_copy(xp_ref, xg_ref.at[pl.ds(0, tm)], sem).wait()


def dispatch(xp, dest, pends, pcounts, *, n_rows):
    n, half = xp.shape
    tm = min(DISPATCH_TILE, n)
    grid_spec = pltpu.PrefetchScalarGridSpec(
        num_scalar_prefetch=3,
        grid=(n // tm,),
        in_specs=[pl.BlockSpec((tm, half), lambda i, *_: (i, 0))],
        out_specs=pl.BlockSpec(memory_space=pl.ANY),
        scratch_shapes=[pltpu.VMEM((MOE_BLOCK, half), U32), pltpu.SemaphoreType.DMA],
    )
    return pl.pallas_call(
        functools.partial(_dispatch_kernel, n_tok=n),
        grid_spec=grid_spec,
        out_shape=jax.ShapeDtypeStruct((n_rows, half), U32),
        compiler_params=_params("arbitrary"),
        name="moe_dispatch",
    )(dest, pends, pcounts, xp)


def _expert_kernel(be_ref, nx_ref, nu_ref, x_ref, wg_hbm, wu_hbm, wd_hbm, y_ref,
                   wg_st, wu_st, wd_st, wgu_s, wd_s, sem, *, layer):
    i = pl.program_id(0)
    used = i < nu_ref[0]
    expert = be_ref[i]
    first = jnp.logical_or(i == 0, expert != be_ref[jnp.maximum(i - 1, 0)])

    def fetch(e):
        return (pltpu.make_async_copy(wg_hbm.at[layer, e], wg_st, sem.at[0]),
                pltpu.make_async_copy(wu_hbm.at[layer, e], wu_st, sem.at[1]),
                pltpu.make_async_copy(wd_hbm.at[layer, e], wd_st, sem.at[2]))

    @pl.when(i == 0)
    def _():
        for c in fetch(expert):
            c.start()

    @pl.when(jnp.logical_and(used, first))
    def _():
        for c in fetch(expert):
            c.wait()
        wgu_s[:, :D_EXPERT] = wg_st[...].astype(BF16)
        wgu_s[:, D_EXPERT:] = wu_st[...].astype(BF16)
        wd_s[...] = wd_st[...].astype(BF16)
        nxt = nx_ref[i]

        @pl.when(nxt >= 0)
        def _():
            for c in fetch(nxt):
                c.start()

    @pl.when(used)
    def _():
        lo, hi = _unpack_rows(x_ref[...])
        x = jnp.concatenate([lo, hi], axis=1).astype(BF16)
        h = jnp.dot(x, wgu_s[...], preferred_element_type=F32)
        hid = (_silu(h[:, :D_EXPERT]) * h[:, D_EXPERT:]).astype(BF16)
        y_ref[...] = _pack_rows(jnp.dot(hid, wd_s[...], preferred_element_type=F32))

    @pl.when(jnp.logical_not(used))
    def _():
        y_ref[...] = jnp.zeros_like(y_ref)


def expert_blocks(xg, blk_e, blk_next, n_used, w_gate, w_up, w_down, layer):
    n_rows, half = xg.shape
    d = 2 * half
    n_blocks = n_rows // MOE_BLOCK

    def xmap(i, be, nx, nu):
        return (jnp.minimum(i, nu[0] - 1), 0)

    hbm = pl.BlockSpec(memory_space=pl.ANY)
    grid_spec = pltpu.PrefetchScalarGridSpec(
        num_scalar_prefetch=3,
        grid=(n_blocks,),
        in_specs=[pl.BlockSpec((MOE_BLOCK, half), xmap), hbm, hbm, hbm],
        out_specs=pl.BlockSpec((MOE_BLOCK, half), lambda i, be, nx, nu: (i, 0)),
        scratch_shapes=[pltpu.VMEM((d, D_EXPERT), F32), pltpu.VMEM((d, D_EXPERT), F32),
                        pltpu.VMEM((D_EXPERT, d), F32),
                        pltpu.VMEM((d, 2 * D_EXPERT), BF16), pltpu.VMEM((D_EXPERT, d), BF16),
                        pltpu.SemaphoreType.DMA((3,))],
    )
    return pl.pallas_call(
        functools.partial(_expert_kernel, layer=layer),
        grid_spec=grid_spec,
        out_shape=jax.ShapeDtypeStruct((n_rows, half), U32),
        compiler_params=_params("arbitrary"),
        name="expert_blocks",
    )(blk_e, blk_next, n_used, xg, w_gate, w_up, w_down)


def _dest_kernel(ps_ref, idx_ref, rank_ref, dest_ref):
    idx = idx_ref[...]
    dest = rank_ref[...]
    for e in range(N_EXPERTS):
        dest = dest + jnp.where(idx == e, ps_ref[e], 0)
    dest_ref[...] = dest


def dispatch_rows(idx, rank, pstarts, *, tn=2048):
    k, n = idx.shape
    tn = min(tn, n)
    spec = pl.BlockSpec((k, tn), lambda i, ps: (0, i))
    return pl.pallas_call(
        _dest_kernel,
        grid_spec=pltpu.PrefetchScalarGridSpec(num_scalar_prefetch=1, grid=(n // tn,),
                                               in_specs=[spec, spec], out_specs=spec),
        out_shape=jax.ShapeDtypeStruct((k, n), I32),
        compiler_params=_params("arbitrary"),
        name="dispatch_rows",
    )(pstarts, idx, rank)


def _combine_kernel(dest_ref, x_ref, xb_ref, gw_ref, sg_ref, su_ref, sd_ref, g_ref, b_ref, yg_ref,
                    y_ref, yb_ref, yp_ref, buf_ref, sem, *, n_tok):
    i = pl.program_id(0)
    n_tiles = pl.num_programs(0)
    tm = x_ref.shape[0]
    slot = i % 2

    def gather(tile, to_slot):
        def issue(t, c):
            for k in range(TOP_K):
                d = dest_ref[k * n_tok + tile * tm + t]
                pltpu.make_async_copy(yg_ref.at[pl.ds(d, 1)], buf_ref.at[to_slot, k, pl.ds(t, 1)],
                                      sem.at[to_slot]).start()
            return c
        lax.fori_loop(0, tm, issue, 0, unroll=8)

    @pl.when(i == 0)
    def _():
        gather(0, 0)

    @pl.when(i + 1 < n_tiles)
    def _():
        gather(i + 1, 1 - slot)

    xb = xb_ref[...]
    hg = jnp.dot(xb, sg_ref[...], preferred_element_type=F32)
    hu = jnp.dot(xb, su_ref[...], preferred_element_type=F32)
    hid = (_silu(hg) * hu).astype(BF16)
    f = jnp.dot(hid, sd_ref[...], preferred_element_type=F32)

    for k in range(TOP_K):
        pltpu.make_async_copy(yg_ref.at[pl.ds(0, tm)], buf_ref.at[slot, k], sem.at[slot]).wait()
    gw = gw_ref[...]
    lo = jnp.zeros((tm, x_ref.shape[1] // 2), F32)
    hi = jnp.zeros((tm, x_ref.shape[1] // 2), F32)
    for k in range(TOP_K):
        l_k, h_k = _unpack_rows(buf_ref[slot, k])
        lo = lo + gw[:, k:k + 1] * l_k
        hi = hi + gw[:, k:k + 1] * h_k
    f = f + jnp.concatenate([lo, hi], axis=1)
    y = _layer_norm_rows(DEEPNORM_ALPHA * x_ref[...] + f, g_ref[...], b_ref[...])
    y_ref[...] = y
    yb_ref[...] = y.astype(BF16)
    yp_ref[...] = _pack_rows(y)


def moe_combine(x, xb, gw, dest, yg, s_gate, s_up, s_down, g, b):
    n, d = x.shape
    tm = min(COMBINE_TILE, n)
    once = pl.Buffered(1)
    row = pl.BlockSpec((tm, d), lambda i, *_: (i, 0))
    prow = pl.BlockSpec((tm, d // 2), lambda i, *_: (i, 0))
    vec = pl.BlockSpec((1, d), lambda i, *_: (0, 0), pipeline_mode=once)
    grid_spec = pltpu.PrefetchScalarGridSpec(
        num_scalar_prefetch=1,
        grid=(n // tm,),
        in_specs=[row, row, pl.BlockSpec((tm, TOP_K), lambda i, *_: (i, 0)),
                  pl.BlockSpec((d, D_EXPERT), lambda i, *_: (0, 0), pipeline_mode=once),
                  pl.BlockSpec((d, D_EXPERT), lambda i, *_: (0, 0), pipeline_mode=once),
                  pl.BlockSpec((D_EXPERT, d), lambda i, *_: (0, 0), pipeline_mode=once),
                  vec, vec, pl.BlockSpec(memory_space=pl.ANY)],
        out_specs=[row, row, prow],
        scratch_shapes=[pltpu.VMEM((2, TOP_K, tm, d // 2), U32), pltpu.SemaphoreType.DMA((2,))],
    )
    return pl.pallas_call(
        functools.partial(_combine_kernel, n_tok=n),
        grid_spec=grid_spec,
        out_shape=[jax.ShapeDtypeStruct((n, d), F32), jax.ShapeDtypeStruct((n, d), BF16),
                   jax.ShapeDtypeStruct((n, d // 2), U32)],
        compiler_params=_params("arbitrary"),
        name="moe_combine",
    )(dest, x, xb, gw, s_gate, s_up, s_down, g.reshape(1, d), b.reshape(1, d), yg)


def moe_ffn(x, xb, xp, layer, w_router, r_bias, w_gate, w_up, w_down, s_gate, s_up, s_down, ln_g, ln_b):
    n_tok = x.shape[0]
    idx, rank, gw, counts = router(x, w_router, r_bias)
    counts = counts.reshape(N_EXPERTS)
    pcounts = (counts + MOE_BLOCK - 1) // MOE_BLOCK * MOE_BLOCK
    pends = jnp.cumsum(pcounts).astype(I32)
    pstarts = pends - pcounts
    n_blocks = n_tok * TOP_K // MOE_BLOCK + N_EXPERTS
    blk_start = jnp.arange(n_blocks, dtype=I32) * MOE_BLOCK
    blk_e = jnp.minimum(jnp.sum(pends[None, :] <= blk_start[:, None], axis=1), N_EXPERTS - 1).astype(I32)
    n_used = (pends[-1:] // MOE_BLOCK).astype(I32)
    eids = jnp.arange(N_EXPERTS, dtype=I32)
    later = jnp.where((eids[None, :] > eids[:, None]) & (pcounts[None, :] > 0), eids[None, :], N_EXPERTS)
    nxt_e = jnp.min(later, axis=1)
    nxt_e = jnp.where(nxt_e == N_EXPERTS, -1, nxt_e)
    blk_next = jnp.sum(jnp.where(blk_e[:, None] == eids[None, :], nxt_e[None, :], 0), axis=1).astype(I32)
    dest = dispatch_rows(idx, rank, pstarts.astype(I32)).reshape(TOP_K * n_tok)

    xg = dispatch(xp, dest, pends, pcounts.astype(I32), n_rows=n_blocks * MOE_BLOCK)
    yg = expert_blocks(xg, blk_e, blk_next, n_used, w_gate, w_up, w_down, layer)
    return moe_combine(x, xb, gw.T, dest, yg, s_gate, s_up, s_down, ln_g, ln_b)


def kernel(x, ln_g, ln_b, ev_w_in, ev_decay_exp, ev_gn_w, ev_rpb, ev_w_out, od_w_in, od_lb, od_norm_w, od_w_out, moe_w_router, moe_bias, moe_w_gate, moe_w_up, moe_w_down, sh_w_gate, sh_w_up, sh_w_down):
    batch, seq, d = x.shape
    n = batch * seq
    xf = x.reshape(n, d).astype(F32)
    xb = xf.astype(BF16)
    for layer in range(DEPTH):
        j = layer // 2
        if layer % 2 == 0:
            ph = matmul(xb, ev_w_in, j, out_dtype=BF16, heads=True)
            o_ret = retention(ph, ev_decay_exp[j], ev_gn_w[j].astype(F32), batch=batch, seq=seq)
            o_na = neighborhood_attention(ph, ev_rpb[j], batch=batch, seq=seq, slab0=4 * RET_W // LANE)
            mixed = jnp.concatenate([o_ret, o_na], axis=1)
            w_out = ev_w_out
        else:
            ph = matmul(xb, od_w_in, j, out_dtype=BF16, heads=True)
            mixed = hgrn2(ph, od_lb, od_norm_w[j].astype(F32), layer=layer, batch=batch, seq=seq)
            w_out = od_w_out
        m = matmul(mixed, w_out, j, out_dtype=F32, heads=False)
        xf, xb, xp = ln_residual(xf, m, ln_g[layer, 0].astype(F32), ln_b[layer, 0].astype(F32))
        xf, xb, xp = moe_ffn(xf, xb, xp, layer, moe_w_router[layer].astype(F32), moe_bias[layer],
                             moe_w_gate, moe_w_up, moe_w_down, sh_w_gate[layer].astype(BF16),
                             sh_w_up[layer].astype(BF16), sh_w_down[layer].astype(BF16),
                             ln_g[layer, 1].astype(F32), ln_b[layer, 1].astype(F32))
    return xf.reshape(batch, seq, d)
```

```python
import functools

import numpy as np
import jax
import jax.numpy as jnp
from jax import lax
from jax.experimental import pallas as pl
from jax.experimental.pallas import tpu as pltpu

F32 = jnp.float32
BF16 = jnp.bfloat16
U32 = jnp.uint32
I32 = jnp.int32

LANE = 128
SUBLANE = 8
VMEM_LIMIT = 56 * 1024 * 1024

DEPTH = 2
GRID_W = 64
RET_HEADS = 8
RET_DK = 256
RET_W = RET_HEADS * RET_DK
RET_CHUNK = 128
ROPE_BASE = 10000.0
NA_HEADS = 16
NA_DH = 128
NA_W = NA_HEADS * NA_DH
NA_KH = 8
NA_KW = 16
NA_UNROLL = 8
HG_HEADS = 32
HG_DK = 128
HG_W = HG_HEADS * HG_DK
HG_CHUNK = 128
HG_SUB = SUBLANE
HG_ROWS = 1024
HG_UNROLL = 8
N_EXPERTS = 64
TOP_K = 8
N_GROUPS = 8
TOPK_GROUPS = 4
E_PER_GROUP = N_EXPERTS // N_GROUPS
D_EXPERT = 384
ROUTED_SCALE = 2.5
MOE_BLOCK = 512
EXPERT_SPLIT = 2
ROUTER_TILE = 512
DISPATCH_TILE = 256
COMBINE_TILE = 128
DEEPNORM_ALPHA = (2.0 * DEPTH) ** 0.25
LN_EPS = 1e-5
NORM_EPS = 1e-6
MASK_NEG = -1e30
LOG2_E = 1.4426950408889634


def _params(*sem):
    return pltpu.CompilerParams(dimension_semantics=sem, vmem_limit_bytes=VMEM_LIMIT)


def _sigmoid_pair(z):
    e = jnp.exp(-jnp.abs(z))
    r = 1.0 / (1.0 + e)
    er = e * r
    pos = z >= 0
    return jnp.where(pos, r, er), jnp.where(pos, er, r)


def _silu(z):
    return z * (0.5 * jnp.tanh(0.5 * z) + 0.5)


def _pack_rows(y):
    half = y.shape[1] // 2
    lo = lax.bitcast_convert_type(y[:, :half].astype(BF16).astype(F32), U32) >> 16
    hi = lax.bitcast_convert_type(y[:, half:].astype(BF16).astype(F32), U32) & jnp.uint32(0xFFFF0000)
    return hi | lo


def _unpack_rows(u):
    lo = lax.bitcast_convert_type(u << 16, F32)
    hi = lax.bitcast_convert_type(u & jnp.uint32(0xFFFF0000), F32)
    return lo, hi


def _mm_kernel(a_ref, w_ref, o_ref, wb_ref, *, heads):
    @pl.when(pl.program_id(1) == 0)
    def _():
        wb_ref[...] = w_ref[...].astype(BF16)

    acc = jnp.dot(a_ref[...], wb_ref[...], preferred_element_type=F32)
    if heads:
        for j in range(o_ref.shape[0]):
            o_ref[j] = acc[:, j * LANE:(j + 1) * LANE].astype(o_ref.dtype)
    else:
        o_ref[...] = acc.astype(o_ref.dtype)


def matmul(a, w, layer, *, out_dtype, heads, tm=1024, tn=512):
    n, k = a.shape
    m = w.shape[2]
    tm, tn = min(tm, n), min(tn, m)
    if heads:
        out_shape = jax.ShapeDtypeStruct((m // LANE, n, LANE), out_dtype)
        out_spec = pl.BlockSpec((tn // LANE, tm, LANE), lambda j, i: (j, i, 0))
    else:
        out_shape = jax.ShapeDtypeStruct((n, m), out_dtype)
        out_spec = pl.BlockSpec((tm, tn), lambda j, i: (i, j))
    return pl.pallas_call(
        functools.partial(_mm_kernel, heads=heads),
        grid=(m // tn, n // tm),
        in_specs=[pl.BlockSpec((tm, k), lambda j, i: (i, 0)),
                  pl.BlockSpec((None, k, tn), lambda j, i: (layer, 0, j))],
        out_specs=out_spec,
        out_shape=out_shape,
        scratch_shapes=[pltpu.VMEM((k, tn), BF16)],
        compiler_params=_params("arbitrary", "arbitrary"),
        name="proj_matmul",
    )(a, w)


def _layer_norm_rows(z, g, b):
    mu = jnp.mean(z, axis=-1, keepdims=True)
    zc = z - mu
    var = jnp.mean(zc * zc, axis=-1, keepdims=True)
    return zc * lax.rsqrt(var + LN_EPS) * g + b


def _ln_kernel(x_ref, m_ref, g_ref, b_ref, y_ref, yb_ref, yp_ref):
    z = DEEPNORM_ALPHA * x_ref[...] + m_ref[...].astype(F32)
    y = _layer_norm_rows(z, g_ref[...], b_ref[...])
    y_ref[...] = y
    yb_ref[...] = y.astype(BF16)
    yp_ref[...] = _pack_rows(y)


def ln_residual(x, m, g, b, *, tm=256):
    n, d = x.shape
    tm = min(tm, n)
    row = pl.BlockSpec((tm, d), lambda i: (i, 0))
    prow = pl.BlockSpec((tm, d // 2), lambda i: (i, 0))
    vec = pl.BlockSpec((1, d), lambda i: (0, 0))
    return pl.pallas_call(
        _ln_kernel,
        grid=(n // tm,),
        in_specs=[row, row, vec, vec],
        out_specs=[row, row, prow],
        out_shape=[jax.ShapeDtypeStruct((n, d), F32), jax.ShapeDtypeStruct((n, d), BF16),
                   jax.ShapeDtypeStruct((n, d // 2), U32)],
        compiler_params=_params("arbitrary"),
        name="ln_residual",
    )(x, m, g.reshape(1, d), b.reshape(1, d))


def _ret_kernel(lg_ref, q_ref, k_ref, v_ref, cos_ref, sin_ref, *rest, rev, n_chunks):
    if rev:
        oa_ref, g_ref, gn_ref, out_ref, s_ref = rest
    else:
        out_ref, s_ref = rest
    C = RET_CHUNK
    h = pl.program_id(1)

    @pl.when(pl.program_id(2) == 0)
    def _():
        s_ref[...] = jnp.zeros_like(s_ref)

    lg = lg_ref[1 if rev else 0, h]
    row = lax.broadcasted_iota(I32, (C, RET_DK), 0).astype(F32)
    if rev:
        q_dec = jnp.exp(lg * (C - row))
        k_dec = jnp.exp(lg * row)
    else:
        q_dec = jnp.exp(lg * (row + 1.0))
        k_dec = jnp.exp(lg * (C - 1.0 - row))
        dist = (lax.broadcasted_iota(I32, (C, C), 0)
                - lax.broadcasted_iota(I32, (C, C), 1)).astype(F32)
        decay = jnp.where(dist >= 0,
                          jnp.exp(lg_ref[0, h] * jnp.maximum(dist, 0.0)),
                          jnp.exp(lg_ref[1, h] * jnp.maximum(-dist, 0.0)))
    chunk_decay = jnp.exp(jnp.full((1, 1), C, F32) * lg)

    def rot(ref, sl, cos, sin):
        t1 = ref[0, sl, :].astype(F32)
        t2 = ref[1, sl, :].astype(F32)
        return jnp.concatenate([t1 * cos - t2 * sin, t1 * sin + t2 * cos], axis=1)

    order = range(n_chunks - 1, -1, -1) if rev else range(n_chunks)
    for ci in order:
        sl = slice(ci * C, (ci + 1) * C)
        cos, sin = cos_ref[sl, :], sin_ref[sl, :]
        qr = rot(q_ref, sl, cos, sin)
        kr = rot(k_ref, sl, cos, sin) * (RET_DK ** -0.5)
        v = jnp.concatenate([v_ref[0, sl, :], v_ref[1, sl, :]], axis=1)
        state = s_ref[...]
        o = jnp.dot((qr * q_dec).astype(BF16), state.astype(BF16), preferred_element_type=F32)
        if rev:
            o = o + jnp.concatenate([oa_ref[0, sl, :], oa_ref[1, sl, :]], axis=1)
            mu = jnp.mean(o, axis=-1, keepdims=True)
            oc = o - mu
            var = jnp.mean(oc * oc, axis=-1, keepdims=True)
            gate = jnp.concatenate([g_ref[0, sl, :], g_ref[1, sl, :]], axis=1).astype(F32)
            res = oc * lax.rsqrt(var + NORM_EPS) * gn_ref[...] * _silu(gate)
            out_ref[sl, :] = res.astype(out_ref.dtype)
        else:
            scores = lax.dot_general(qr.astype(BF16), kr.astype(BF16), (((1,), (1,)), ((), ())),
                                     preferred_element_type=F32) * decay
            o = o + jnp.dot(scores.astype(BF16), v, preferred_element_type=F32)
            out_ref[0, sl, :] = o[:, :LANE]
            out_ref[1, sl, :] = o[:, LANE:]
        kv = lax.dot_general((kr * k_dec).astype(BF16), v, (((0,), (0,)), ((), ())),
                             preferred_element_type=F32)
        s_ref[...] = chunk_decay * state + kv


def _ret_pass(lg, ph, cos, sin, extra, *, rev, batch, seq, rows):
    n_blk = seq // rows
    n_chunks = rows // RET_CHUNK
    H = RET_HEADS

    def blk(step):
        return (n_blk - 1 - step) if rev else step

    def slab(group):
        return pl.BlockSpec((2, rows, LANE), lambda b, h, s: (group * H + h, b * n_blk + blk(s), 0))

    tab = pl.BlockSpec((rows, LANE), lambda b, h, s: (blk(s), 0))
    in_specs = [pl.BlockSpec(memory_space=pltpu.SMEM), slab(0), slab(1), slab(2), tab, tab]
    args = [lg, ph, ph, ph, cos, sin]
    if rev:
        oa, gn_w = extra
        in_specs += [pl.BlockSpec((2, rows, LANE), lambda b, h, s: (h, b * n_blk + blk(s), 0)),
                     slab(3),
                     pl.BlockSpec((1, RET_DK), lambda b, h, s: (0, h))]
        args += [oa, ph, gn_w.reshape(1, RET_W)]
        out_shape = jax.ShapeDtypeStruct((batch * seq, RET_W), BF16)
        out_spec = pl.BlockSpec((rows, RET_DK), lambda b, h, s: (b * n_blk + blk(s), h))
    else:
        out_shape = jax.ShapeDtypeStruct((2 * H, batch * seq, LANE), F32)
        out_spec = pl.BlockSpec((2, rows, LANE), lambda b, h, s: (h, b * n_blk + blk(s), 0))
    return pl.pallas_call(
        functools.partial(_ret_kernel, rev=rev, n_chunks=n_chunks),
        grid=(batch, H, n_blk),
        in_specs=in_specs,
        out_specs=out_spec,
        out_shape=out_shape,
        scratch_shapes=[pltpu.VMEM((RET_DK, RET_DK), F32)],
        compiler_params=_params("arbitrary", "arbitrary", "arbitrary"),
        name="retention_bwd" if rev else "retention_fwd",
    )(*args)


def retention(ph, decay_exp, gn_w, *, batch, seq, rows=512):
    rows = min(rows, seq)
    log_g = jnp.log1p(-jnp.exp2(-decay_exp.astype(F32)))
    inv = ROPE_BASE ** (-jnp.linspace(0.0, 1.0, RET_DK // 2, dtype=F32))
    ang = jnp.arange(seq, dtype=F32)[:, None] * inv[None, :]
    cos, sin = jnp.cos(ang), jnp.sin(ang)
    o_a = _ret_pass(log_g, ph, cos, sin, None, rev=False, batch=batch, seq=seq, rows=rows)
    return _ret_pass(log_g, ph, cos, sin, (o_a, gn_w), rev=True, batch=batch, seq=seq, rows=rows)


def _na_kernel(q_ref, k_ref, v_ref, tab_ref, o_ref, *, grid_rows):
    span = NA_KH * GRID_W
    scale = NA_DH ** -0.5

    def body(it, carry):
        rows = [it * NA_UNROLL + j for j in range(NA_UNROLL)]
        starts = [jnp.clip(r - NA_KH // 2, 0, grid_rows - NA_KH) for r in rows]
        scores = []
        for r, r0 in zip(rows, starts):
            q = q_ref[0, pl.ds(pl.multiple_of(r * GRID_W, GRID_W), GRID_W), :]
            kw = k_ref[0, pl.ds(pl.multiple_of(r0 * GRID_W, GRID_W), span), :]
            s = lax.dot_general(q, kw, (((1,), (1,)), ((), ())), preferred_element_type=F32)
            scores.append(s * scale + tab_ref[0, r0 - r + NA_KH - 1])
        probs = []
        for s in scores:
            p = jnp.exp(s - jnp.max(s, axis=-1, keepdims=True))
            probs.append((p.astype(BF16), jnp.sum(p, axis=-1, keepdims=True)))
        for r, r0, (p, l) in zip(rows, starts, probs):
            vw = v_ref[0, pl.ds(pl.multiple_of(r0 * GRID_W, GRID_W), span), :]
            o = jnp.dot(p, vw, preferred_element_type=F32) / l
            o_ref[pl.ds(pl.multiple_of(r * GRID_W, GRID_W), GRID_W), :] = o.astype(o_ref.dtype)
        return carry

    lax.fori_loop(0, grid_rows // NA_UNROLL, body, 0)


def _na_bias_table(rpb):
    qc = np.arange(GRID_W)[:, None]
    kc = np.arange(GRID_W)[None, :]
    wstart = np.clip(qc - NA_KW // 2, 0, GRID_W - NA_KW)
    valid = (kc >= wstart) & (kc < wstart + NA_KW)
    dc = np.clip(kc - qc, -(NA_KW - 1), NA_KW - 1) + NA_KW - 1
    onehot = ((dc[None] == np.arange(2 * NA_KW - 1)[:, None, None]) & valid[None]).astype(np.float32)
    by_col = jnp.einsum('hrj,jck->hrck', rpb, jnp.asarray(onehot), precision=lax.Precision.HIGHEST)
    by_col = by_col + jnp.asarray(np.where(valid, 0.0, MASK_NEG).astype(np.float32))[None, None]
    tab = jnp.stack([by_col[:, off:off + NA_KH] for off in range(NA_KH)], axis=1)
    return tab.transpose(0, 1, 3, 2, 4).reshape(NA_HEADS, NA_KH, GRID_W, NA_KH * GRID_W)


def neighborhood_attention(ph, rpb, *, batch, seq, slab0):
    grid_rows = seq // GRID_W
    assert grid_rows >= NA_KH and grid_rows % NA_UNROLL == 0
    tab = _na_bias_table(rpb.astype(F32))

    def slab(group):
        return pl.BlockSpec((1, seq, LANE), lambda b, h: (slab0 + group * NA_HEADS + h, b, 0))

    return pl.pallas_call(
        functools.partial(_na_kernel, grid_rows=grid_rows),
        grid=(batch, NA_HEADS),
        in_specs=[slab(0), slab(1), slab(2),
                  pl.BlockSpec((1, NA_KH, GRID_W, NA_KH * GRID_W), lambda b, h: (h, 0, 0, 0))],
        out_specs=pl.BlockSpec((seq, NA_DH), lambda b, h: (b, h)),
        out_shape=jax.ShapeDtypeStruct((batch * seq, NA_W), BF16),
        compiler_params=_params("arbitrary", "arbitrary"),
        name="neighborhood_attention",
    )(ph, ph, ph, tab)


def _hg_widths():
    widths, w = [], HG_CHUNK // 2
    while w >= HG_SUB:
        widths.append(w)
        w //= 2
    return widths


def _hg_kernel(lb_ref, zq_ref, zf_ref, zi_ref, *rest, rev, layer, n_chunks):
    if rev:
        of_ref, zg_ref, nw_ref, out_ref, st_ref, lvl_ref, dg_ref = rest
    else:
        out_ref, st_ref, lvl_ref, dg_ref = rest
    C = HG_CHUNK
    U = HG_SUB
    widths = _hg_widths()

    @pl.when(pl.program_id(2) == 0)
    def _():
        st_ref[...] = jnp.zeros_like(st_ref)
        s_i = lax.broadcasted_iota(I32, (C, C), 0)
        t_i = lax.broadcasted_iota(I32, (C, C), 1)
        earlier = (s_i > t_i) if rev else (s_i < t_i)
        split = jnp.where(earlier, s_i ^ t_i, 0)
        for l, w in enumerate(widths):
            lvl_ref[l] = jnp.where(split >= w, jnp.where(split < 2 * w, 1.0, 0.0), 0.0)
        near = jnp.where((s_i >= t_i) if rev else (s_i <= t_i), jnp.where((s_i ^ t_i) < U, 1.0, 0.0), 0.0)
        for tt in range(U):
            dg_ref[tt] = near * jnp.where((t_i & (U - 1)) == tt, 1.0, 0.0)

    lbp = lb_ref[...]
    pe = jnp.exp(lbp - jnp.max(lbp, axis=0, keepdims=True))
    share = pe / jnp.sum(pe, axis=0, keepdims=True)
    lb = jnp.sum(share[:layer + 1], axis=0, keepdims=True) - share[0:1]
    one_m_lb = 1.0 - lb

    ri = lax.broadcasted_iota(I32, (C, C), 0)
    ci_ = lax.broadcasted_iota(I32, (C, C), 1)
    tri = jnp.where((ri <= ci_) if rev else (ri >= ci_), 1.0, 0.0).astype(BF16)
    row = lax.broadcasted_iota(I32, (C, HG_DK), 0)

    def rows_of(step):
        cidx = (n_chunks - 1 - step) if rev else step
        return pl.ds(pl.multiple_of(cidx * C, C), C)

    def local_part(sl):
        q = _silu(zq_ref[0, sl, :].astype(F32))
        sig, nsig = _sigmoid_pair(zf_ref[0, sl, :].astype(F32))
        k = one_m_lb * nsig
        lf = jnp.log(lb + one_m_lb * sig) * LOG2_E
        v = zi_ref[0, sl, :]

        hi = lf.astype(BF16)
        r1 = lf - hi.astype(F32)
        mid = r1.astype(BF16)
        lo = (r1 - mid.astype(F32)).astype(BF16)
        parts = jnp.dot(tri, jnp.concatenate([hi, mid, lo], axis=1), preferred_element_type=F32)
        b = parts[:, :LANE] + parts[:, LANE:2 * LANE] + parts[:, 2 * LANE:]

        att_t = jnp.zeros((C, C), F32)
        for l, w in enumerate(widths):
            b3 = b.reshape(C // (2 * w), 2 * w, HG_DK)
            edge = b3[:, w:w + 1, :] if rev else b3[:, w - 1:w, :]
            d = (b3 - edge).reshape(C, HG_DK)
            is_q = ((row & w) == 0) if rev else ((row & w) != 0)
            x = (jnp.where(is_q, q, k) * jnp.exp2(jnp.where(is_q, d, -d))).astype(BF16)
            a = lax.dot_general(x, x, (((1,), (1,)), ((), ())), preferred_element_type=F32)
            att_t = att_t + a * lvl_ref[l]

        nu = C // U
        b8 = b.reshape(nu, U, HG_DK)
        q8 = q.reshape(nu, U, HG_DK)
        k8 = k.reshape(nu, U, HG_DK)
        for tt in range(U):
            e = jnp.exp2(jnp.minimum(b8[:, tt:tt + 1, :] - b8, 0.0))
            a = jnp.sum(e * k8 * q8[:, tt:tt + 1, :], axis=-1, keepdims=True)
            att_t = att_t + a.reshape(C, 1) * dg_ref[tt]
        o = lax.dot_general(att_t.astype(BF16), v, (((0,), (0,)), ((), ())), preferred_element_type=F32)

        b_end = b[0:1, :] if rev else b[C - 1:C, :]
        kd = (k * jnp.exp2(b_end - b)).astype(BF16)
        kv = lax.dot_general(v, kd, (((0,), (0,)), ((), ())), preferred_element_type=F32)
        return o, (q * jnp.exp2(b)).astype(BF16), kv, jnp.exp2(b_end)

    def carried_part(sl, local):
        o, q_dec, kv, end_decay = local
        state_t = st_ref[...]
        o = o + lax.dot_general(q_dec, state_t.astype(BF16), (((1,), (1,)), ((), ())),
                                preferred_element_type=F32)
        st_ref[...] = state_t * end_decay + kv
        if rev:
            o = o + of_ref[0, sl, :]
            o = o * lax.rsqrt(jnp.mean(o * o, axis=-1, keepdims=True) + NORM_EPS)
            res = o * nw_ref[...] * _silu(zg_ref[0, sl, :].astype(F32))
            out_ref[sl, :] = res.astype(out_ref.dtype)
        else:
            out_ref[0, sl, :] = o

    def group(it, carry):
        slices = [rows_of(it * HG_UNROLL + j) for j in range(HG_UNROLL)]
        locals_ = [local_part(sl) for sl in slices]
        for sl, local in zip(slices, locals_):
            carried_part(sl, local)
        return carry

    lax.fori_loop(0, n_chunks // HG_UNROLL, group, 0)


def _hg_pass(od_lb, ph, extra, *, rev, layer, batch, seq, rows):
    n_blk = seq // rows
    H = HG_HEADS

    def blk(step):
        return (n_blk - 1 - step) if rev else step

    def slab(group):
        return pl.BlockSpec((1, rows, LANE), lambda b, h, s: (group * H + h, b * n_blk + blk(s), 0))

    in_specs = [pl.BlockSpec((od_lb.shape[0], HG_DK), lambda b, h, s: (0, h)),
                slab(0), slab(2 if rev else 1), slab(3)]
    args = [od_lb, ph, ph, ph]
    if rev:
        o_f, norm_w = extra
        in_specs += [pl.BlockSpec((1, rows, LANE), lambda b, h, s: (h, b * n_blk + blk(s), 0)),
                     slab(4),
                     pl.BlockSpec((1, HG_DK), lambda b, h, s: (0, h))]
        args += [o_f, ph, norm_w.reshape(1, HG_W)]
        out_shape = jax.ShapeDtypeStruct((batch * seq, HG_W), BF16)
        out_spec = pl.BlockSpec((rows, HG_DK), lambda b, h, s: (b * n_blk + blk(s), h))
    else:
        out_shape = jax.ShapeDtypeStruct((H, batch * seq, LANE), F32)
        out_spec = pl.BlockSpec((1, rows, LANE), lambda b, h, s: (h, b * n_blk + blk(s), 0))
    return pl.pallas_call(
        functools.partial(_hg_kernel, rev=rev, layer=layer, n_chunks=rows // HG_CHUNK),
        grid=(batch, H, n_blk),
        in_specs=in_specs,
        out_specs=out_spec,
        out_shape=out_shape,
        scratch_shapes=[pltpu.VMEM((HG_DK, HG_DK), F32),
                        pltpu.VMEM((len(_hg_widths()), HG_CHUNK, HG_CHUNK), F32),
                        pltpu.VMEM((HG_SUB, HG_CHUNK, HG_CHUNK), F32)],
        compiler_params=_params("arbitrary", "arbitrary", "arbitrary"),
        name="hgrn2_bwd" if rev else "hgrn2_fwd",
    )(*args)


def hgrn2(ph, od_lb, norm_w, *, layer, batch, seq):
    rows = min(HG_ROWS, seq)
    od_lb = od_lb.astype(F32)
    o_f = _hg_pass(od_lb, ph, None, rev=False, layer=layer, batch=batch, seq=seq, rows=rows)
    return _hg_pass(od_lb, ph, (o_f, norm_w), rev=True, layer=layer, batch=batch, seq=seq, rows=rows)


def _router_kernel(wt_ref, bias_ref, tri_ref, x_ref, idx_ref, rank_ref, gw_ref, cnt_ref, run_ref):
    tm = x_ref.shape[0]
    G, P = N_GROUPS, E_PER_GROUP

    @pl.when(pl.program_id(0) == 0)
    def _():
        run_ref[...] = jnp.zeros_like(run_ref)

    logits = lax.dot_general(wt_ref[...], x_ref[...], (((1,), (1,)), ((), ())),
                             precision=lax.Precision.HIGHEST, preferred_element_type=F32)
    scores = _sigmoid_pair(logits)[0]
    sc3 = scores.reshape(G, P, tm)
    x3 = (scores + bias_ref[...]).reshape(G, P, tm)
    neg = -jnp.inf

    def max01(a):
        return jnp.max(jnp.max(a, axis=0, keepdims=True), axis=1, keepdims=True)

    def min01(a):
        return jnp.min(jnp.min(a, axis=0, keepdims=True), axis=1, keepdims=True)

    def sum01(a):
        return jnp.sum(jnp.sum(a, axis=0, keepdims=True), axis=1, keepdims=True)

    pidx = lax.broadcasted_iota(I32, (G, P, tm), 1)
    m1 = jnp.max(x3, axis=1, keepdims=True)
    i1 = jnp.min(jnp.where(x3 == m1, pidx, P), axis=1, keepdims=True)
    m2 = jnp.max(jnp.where(pidx == i1, neg, x3), axis=1, keepdims=True)
    gs = m1 + m2
    gidx = lax.broadcasted_iota(I32, (G, 1, tm), 0)
    before = jnp.zeros((G, 1, tm), I32)
    for g in range(G):
        other = gs[g:g + 1]
        ahead = jnp.where(other > gs, 1, jnp.where(other == gs, jnp.where(gidx > g, 1, 0), 0))
        before = before + ahead
    xm = jnp.where(before < TOPK_GROUPS, x3, neg)

    eidx = lax.broadcasted_iota(I32, (G, P, tm), 0) * P + pidx
    picks = []
    sel = jnp.zeros((G, P, tm), F32)
    for _ in range(TOP_K):
        m = max01(xm)
        ik = min01(jnp.where(xm == m, eidx, N_EXPERTS))
        hit = eidx == ik
        xm = jnp.where(hit, neg, xm)
        sel = jnp.where(hit, 1.0, sel)
        picks.append(ik)

    sel2 = sel.reshape(N_EXPERTS, tm)
    prefix = jnp.dot(sel2.astype(BF16), tri_ref[...], preferred_element_type=F32)
    rank3 = (run_ref[...] + prefix).reshape(G, P, tm)
    run_ref[...] = run_ref[...] + jnp.sum(sel2, axis=1, keepdims=True)
    cnt_ref[...] = run_ref[...].astype(I32)

    krow = lax.broadcasted_iota(I32, (TOP_K, tm), 0)
    idx_o = jnp.zeros((TOP_K, tm), I32)
    rank_o = jnp.zeros((TOP_K, tm), F32)
    gw_o = jnp.zeros((TOP_K, tm), F32)
    for kk, ik in enumerate(picks):
        hit = eidx == ik
        rk = sum01(jnp.where(hit, rank3, 0.0)).reshape(1, tm)
        sk = sum01(jnp.where(hit, sc3, 0.0)).reshape(1, tm)
        idx_o = jnp.where(krow == kk, ik.reshape(1, tm), idx_o)
        rank_o = jnp.where(krow == kk, rk, rank_o)
        gw_o = jnp.where(krow == kk, sk, gw_o)
    gw_o = gw_o / jnp.sum(gw_o, axis=0, keepdims=True) * ROUTED_SCALE
    idx_ref[...] = idx_o
    rank_ref[...] = rank_o.astype(I32)
    gw_ref[...] = gw_o


def router(x, w_router, r_bias):
    n, d = x.shape
    tm = min(ROUTER_TILE, n)
    tri = jnp.triu(jnp.ones((tm, tm), BF16), k=1)
    kspec = pl.BlockSpec((TOP_K, tm), lambda i: (0, i))
    return pl.pallas_call(
        _router_kernel,
        grid=(n // tm,),
        in_specs=[pl.BlockSpec((N_EXPERTS, d), lambda i: (0, 0)),
                  pl.BlockSpec((N_EXPERTS, 1), lambda i: (0, 0)),
                  pl.BlockSpec((tm, tm), lambda i: (0, 0)),
                  pl.BlockSpec((tm, d), lambda i: (i, 0))],
        out_specs=[kspec, kspec, kspec, pl.BlockSpec((N_EXPERTS, 1), lambda i: (0, 0))],
        out_shape=[jax.ShapeDtypeStruct((TOP_K, n), I32), jax.ShapeDtypeStruct((TOP_K, n), I32),
                   jax.ShapeDtypeStruct((TOP_K, n), F32), jax.ShapeDtypeStruct((N_EXPERTS, 1), I32)],
        scratch_shapes=[pltpu.VMEM((N_EXPERTS, 1), F32)],
        compiler_params=_params("arbitrary"),
        name="router",
    )(w_router.T, r_bias.astype(F32).reshape(N_EXPERTS, 1), tri, x)


def _dispatch_kernel(dest_ref, pend_ref, pcnt_ref, xp_ref, xg_ref, zero_ref, sem, *, n_tok):
    i = pl.program_id(0)
    tm = xp_ref.shape[0]

    def zero_block(start):
        return pltpu.make_async_copy(zero_ref, xg_ref.at[pl.ds(pl.multiple_of(start, MOE_BLOCK), MOE_BLOCK)], sem)

    @pl.when(i == 0)
    def _():
        zero_ref[...] = jnp.zeros_like(zero_ref)
        n_blocks = xg_ref.shape[0] // MOE_BLOCK
        n_used = pend_ref[N_EXPERTS - 1] // MOE_BLOCK

        def start(e, c):
            @pl.when(pcnt_ref[e] > 0)
            def _():
                zero_block(pend_ref[e] - MOE_BLOCK).start()
            return c

        def wait(e, c):
            @pl.when(pcnt_ref[e] > 0)
            def _():
                zero_block(pend_ref[e] - MOE_BLOCK).wait()
            return c

        def start_tail(blk, c):
            zero_block(blk * MOE_BLOCK).start()
            return c

        def wait_tail(blk, c):
            zero_block(blk * MOE_BLOCK).wait()
            return c

        lax.fori_loop(0, N_EXPERTS, start, 0)
        lax.fori_loop(n_used, n_blocks, start_tail, 0)
        lax.fori_loop(0, N_EXPERTS, wait, 0)
        lax.fori_loop(n_used, n_blocks, wait_tail, 0)

    def row_copy(t, k):
        d = dest_ref[k * n_tok + i * tm + t]
        return pltpu.make_async_copy(xp_ref.at[pl.ds(t, 1)], xg_ref.at[pl.ds(d, 1)], sem)

    def issue(t, c):
        for k in range(TOP_K):
            row_copy(t, k).start()
        return c

    lax.fori_loop(0, tm, issue, 0, unroll=8)
    for _ in range(TOP_K):
        pltpu.make_async_copy(xp_ref, xg_ref.at[pl.ds(0, tm)], sem).wait()


def dispatch(xp, dest, pends, pcounts, *, n_rows):
    n, half = xp.shape
    tm = min(DISPATCH_TILE, n)
    grid_spec = pltpu.PrefetchScalarGridSpec(
        num_scalar_prefetch=3,
        grid=(n // tm,),
        in_specs=[pl.BlockSpec((tm, half), lambda i, *_: (i, 0))],
        out_specs=pl.BlockSpec(memory_space=pl.ANY),
        scratch_shapes=[pltpu.VMEM((MOE_BLOCK, half), U32), pltpu.SemaphoreType.DMA],
    )
    return pl.pallas_call(
        functools.partial(_dispatch_kernel, n_tok=n),
        grid_spec=grid_spec,
        out_shape=jax.ShapeDtypeStruct((n_rows, half), U32),
        compiler_params=_params("arbitrary"),
        name="moe_dispatch",
    )(dest, pends, pcounts, xp)


def _expert_kernel(be_ref, nx_ref, nu_ref, x_ref, wg_hbm, wu_hbm, wd_hbm, y_ref,
                   wg_st, wu_st, wd_st, wgu_s, wd_s, sem, *, layer):
    i = pl.program_id(0)
    used = i < nu_ref[0]
    expert = be_ref[i]
    first = jnp.logical_or(i == 0, expert != be_ref[jnp.maximum(i - 1, 0)])

    def fetch(e):
        return (pltpu.make_async_copy(wg_hbm.at[layer, e], wg_st, sem.at[0]),
                pltpu.make_async_copy(wu_hbm.at[layer, e], wu_st, sem.at[1]),
                pltpu.make_async_copy(wd_hbm.at[layer, e], wd_st, sem.at[2]))

    @pl.when(i == 0)
    def _():
        for c in fetch(expert):
            c.start()

    @pl.when(jnp.logical_and(used, first))
    def _():
        for c in fetch(expert):
            c.wait()
        wgu_s[:, :D_EXPERT] = wg_st[...].astype(BF16)
        wgu_s[:, D_EXPERT:] = wu_st[...].astype(BF16)
        wd_s[...] = wd_st[...].astype(BF16)
        nxt = nx_ref[i]

        @pl.when(nxt >= 0)
        def _():
            for c in fetch(nxt):
                c.start()

    @pl.when(used)
    def _():
        sub = MOE_BLOCK // EXPERT_SPLIT
        hs = []
        for r in range(EXPERT_SPLIT):
            lo, hi = _unpack_rows(x_ref[r * sub:(r + 1) * sub, :])
            x = jnp.concatenate([lo, hi], axis=1).astype(BF16)
            hs.append(jnp.dot(x, wgu_s[...], preferred_element_type=F32))
        for r, h in enumerate(hs):
            hid = (_silu(h[:, :D_EXPERT]) * h[:, D_EXPERT:]).astype(BF16)
            y_ref[r * sub:(r + 1) * sub, :] = _pack_rows(jnp.dot(hid, wd_s[...], preferred_element_type=F32))

    @pl.when(jnp.logical_not(used))
    def _():
        y_ref[...] = jnp.zeros_like(y_ref)


def expert_blocks(xg, blk_e, blk_next, n_used, w_gate, w_up, w_down, layer):
    n_rows, half = xg.shape
    d = 2 * half
    n_blocks = n_rows // MOE_BLOCK

    def xmap(i, be, nx, nu):
        return (jnp.minimum(i, nu[0] - 1), 0)

    hbm = pl.BlockSpec(memory_space=pl.ANY)
    grid_spec = pltpu.PrefetchScalarGridSpec(
        num_scalar_prefetch=3,
        grid=(n_blocks,),
        in_specs=[pl.BlockSpec((MOE_BLOCK, half), xmap), hbm, hbm, hbm],
        out_specs=pl.BlockSpec((MOE_BLOCK, half), lambda i, be, nx, nu: (i, 0)),
        scratch_shapes=[pltpu.VMEM((d, D_EXPERT), F32), pltpu.VMEM((d, D_EXPERT), F32),
                        pltpu.VMEM((D_EXPERT, d), F32),
                        pltpu.VMEM((d, 2 * D_EXPERT), BF16), pltpu.VMEM((D_EXPERT, d), BF16),
                        pltpu.SemaphoreType.DMA((3,))],
    )
    return pl.pallas_call(
        functools.partial(_expert_kernel, layer=layer),
        grid_spec=grid_spec,
        out_shape=jax.ShapeDtypeStruct((n_rows, half), U32),
        compiler_params=_params("arbitrary"),
        name="expert_blocks",
    )(blk_e, blk_next, n_used, xg, w_gate, w_up, w_down)


def _dest_kernel(ps_ref, idx_ref, rank_ref, dest_ref):
    idx = idx_ref[...]
    dest = rank_ref[...]
    for e in range(N_EXPERTS):
        dest = dest + jnp.where(idx == e, ps_ref[e], 0)
    dest_ref[...] = dest


def dispatch_rows(idx, rank, pstarts, *, tn=2048):
    k, n = idx.shape
    tn = min(tn, n)
    spec = pl.BlockSpec((k, tn), lambda i, ps: (0, i))
    return pl.pallas_call(
        _dest_kernel,
        grid_spec=pltpu.PrefetchScalarGridSpec(num_scalar_prefetch=1, grid=(n // tn,),
                                               in_specs=[spec, spec], out_specs=spec),
        out_shape=jax.ShapeDtypeStruct((k, n), I32),
        compiler_params=_params("arbitrary"),
        name="dispatch_rows",
    )(pstarts, idx, rank)


def _combine_kernel(dest_ref, x_ref, xb_ref, gw_ref, sg_ref, su_ref, sd_ref, g_ref, b_ref, yg_ref,
                    y_ref, yb_ref, yp_ref, buf_ref, sem, *, n_tok):
    i = pl.program_id(0)
    n_tiles = pl.num_programs(0)
    tm = x_ref.shape[0]
    slot = i % 2

    def gather(tile, to_slot):
        def issue(t, c):
            for k in range(TOP_K):
                d = dest_ref[k * n_tok + tile * tm + t]
                pltpu.make_async_copy(yg_ref.at[pl.ds(d, 1)], buf_ref.at[to_slot, k, pl.ds(t, 1)],
                                      sem.at[to_slot]).start()
            return c
        lax.fori_loop(0, tm, issue, 0, unroll=8)

    @pl.when(i == 0)
    def _():
        gather(0, 0)

    @pl.when(i + 1 < n_tiles)
    def _():
        gather(i + 1, 1 - slot)

    xb = xb_ref[...]
    hg = jnp.dot(xb, sg_ref[...], preferred_element_type=F32)
    hu = jnp.dot(xb, su_ref[...], preferred_element_type=F32)
    hid = (_silu(hg) * hu).astype(BF16)
    f = jnp.dot(hid, sd_ref[...], preferred_element_type=F32)

    for k in range(TOP_K):
        pltpu.make_async_copy(yg_ref.at[pl.ds(0, tm)], buf_ref.at[slot, k], sem.at[slot]).wait()
    gw = gw_ref[...]
    lo = jnp.zeros((tm, x_ref.shape[1] // 2), F32)
    hi = jnp.zeros((tm, x_ref.shape[1] // 2), F32)
    for k in range(TOP_K):
        l_k, h_k = _unpack_rows(buf_ref[slot, k])
        lo = lo + gw[:, k:k + 1] * l_k
        hi = hi + gw[:, k:k + 1] * h_k
    f = f + jnp.concatenate([lo, hi], axis=1)
    y = _layer_norm_rows(DEEPNORM_ALPHA * x_ref[...] + f, g_ref[...], b_ref[...])
    y_ref[...] = y
    yb_ref[...] = y.astype(BF16)
    yp_ref[...] = _pack_rows(y)


def moe_combine(x, xb, gw, dest, yg, s_gate, s_up, s_down, g, b):
    n, d = x.shape
    tm = min(COMBINE_TILE, n)
    once = pl.Buffered(1)
    row = pl.BlockSpec((tm, d), lambda i, *_: (i, 0))
    prow = pl.BlockSpec((tm, d // 2), lambda i, *_: (i, 0))
    vec = pl.BlockSpec((1, d), lambda i, *_: (0, 0), pipeline_mode=once)
    grid_spec = pltpu.PrefetchScalarGridSpec(
        num_scalar_prefetch=1,
        grid=(n // tm,),
        in_specs=[row, row, pl.BlockSpec((tm, TOP_K), lambda i, *_: (i, 0)),
                  pl.BlockSpec((d, D_EXPERT), lambda i, *_: (0, 0), pipeline_mode=once),
                  pl.BlockSpec((d, D_EXPERT), lambda i, *_: (0, 0), pipeline_mode=once),
                  pl.BlockSpec((D_EXPERT, d), lambda i, *_: (0, 0), pipeline_mode=once),
                  vec, vec, pl.BlockSpec(memory_space=pl.ANY)],
        out_specs=[row, row, prow],
        scratch_shapes=[pltpu.VMEM((2, TOP_K, tm, d // 2), U32), pltpu.SemaphoreType.DMA((2,))],
    )
    return pl.pallas_call(
        functools.partial(_combine_kernel, n_tok=n),
        grid_spec=grid_spec,
        out_shape=[jax.ShapeDtypeStruct((n, d), F32), jax.ShapeDtypeStruct((n, d), BF16),
                   jax.ShapeDtypeStruct((n, d // 2), U32)],
        compiler_params=_params("arbitrary"),
        name="moe_combine",
    )(dest, x, xb, gw, s_gate, s_up, s_down, g.reshape(1, d), b.reshape(1, d), yg)


def moe_ffn(x, xb, xp, layer, w_router, r_bias, w_gate, w_up, w_down, s_gate, s_up, s_down, ln_g, ln_b):
    n_tok = x.shape[0]
    idx, rank, gw, counts = router(x, w_router, r_bias)
    counts = counts.reshape(N_EXPERTS)
    pcounts = (counts + MOE_BLOCK - 1) // MOE_BLOCK * MOE_BLOCK
    pends = jnp.cumsum(pcounts).astype(I32)
    pstarts = pends - pcounts
    n_blocks = n_tok * TOP_K // MOE_BLOCK + N_EXPERTS
    blk_start = jnp.arange(n_blocks, dtype=I32) * MOE_BLOCK
    blk_e = jnp.minimum(jnp.sum(pends[None, :] <= blk_start[:, None], axis=1), N_EXPERTS - 1).astype(I32)
    n_used = (pends[-1:] // MOE_BLOCK).astype(I32)
    eids = jnp.arange(N_EXPERTS, dtype=I32)
    later = jnp.where((eids[None, :] > eids[:, None]) & (pcounts[None, :] > 0), eids[None, :], N_EXPERTS)
    nxt_e = jnp.min(later, axis=1)
    nxt_e = jnp.where(nxt_e == N_EXPERTS, -1, nxt_e)
    blk_next = jnp.sum(jnp.where(blk_e[:, None] == eids[None, :], nxt_e[None, :], 0), axis=1).astype(I32)
    dest = dispatch_rows(idx, rank, pstarts.astype(I32)).reshape(TOP_K * n_tok)

    xg = dispatch(xp, dest, pends, pcounts.astype(I32), n_rows=n_blocks * MOE_BLOCK)
    yg = expert_blocks(xg, blk_e, blk_next, n_used, w_gate, w_up, w_down, layer)
    return moe_combine(x, xb, gw.T, dest, yg, s_gate, s_up, s_down, ln_g, ln_b)


def kernel(x, ln_g, ln_b, ev_w_in, ev_decay_exp, ev_gn_w, ev_rpb, ev_w_out, od_w_in, od_lb, od_norm_w, od_w_out, moe_w_router, moe_bias, moe_w_gate, moe_w_up, moe_w_down, sh_w_gate, sh_w_up, sh_w_down):
    batch, seq, d = x.shape
    n = batch * seq
    xf = x.reshape(n, d).astype(F32)
    xb = xf.astype(BF16)
    for layer in range(DEPTH):
        j = layer // 2
        if layer % 2 == 0:
            ph = matmul(xb, ev_w_in, j, out_dtype=BF16, heads=True)
            o_ret = retention(ph, ev_decay_exp[j], ev_gn_w[j].astype(F32), batch=batch, seq=seq)
            o_na = neighborhood_attention(ph, ev_rpb[j], batch=batch, seq=seq, slab0=4 * RET_W // LANE)
            mixed = jnp.concatenate([o_ret, o_na], axis=1)
            w_out = ev_w_out
        else:
            ph = matmul(xb, od_w_in, j, out_dtype=BF16, heads=True)
            mixed = hgrn2(ph, od_lb, od_norm_w[j].astype(F32), layer=layer, batch=batch, seq=seq)
            w_out = od_w_out
        m = matmul(mixed, w_out, j, out_dtype=F32, heads=False)
        xf, xb, xp = ln_residual(xf, m, ln_g[layer, 0].astype(F32), ln_b[layer, 0].astype(F32))
        xf, xb, xp = moe_ffn(xf, xb, xp, layer, moe_w_router[layer].astype(F32), moe_bias[layer],
                             moe_w_gate, moe_w_up, moe_w_down, sh_w_gate[layer].astype(BF16),
                             sh_w_up[layer].astype(BF16), sh_w_down[layer].astype(BF16),
                             ln_g[layer, 1].astype(F32), ln_b[layer, 1].astype(F32))
    return xf.reshape(batch, seq, d)
```

```python
import functools

import numpy as np
import jax
import jax.numpy as jnp
from jax import lax
from jax.experimental import pallas as pl
from jax.experimental.pallas import tpu as pltpu

F32 = jnp.float32
BF16 = jnp.bfloat16
U32 = jnp.uint32
I32 = jnp.int32

LANE = 128
SUBLANE = 8
VMEM_LIMIT = 56 * 1024 * 1024

DEPTH = 2
GRID_W = 64
RET_HEADS = 8
RET_DK = 256
RET_W = RET_HEADS * RET_DK
RET_CHUNK = 128
ROPE_BASE = 10000.0
NA_HEADS = 16
NA_DH = 128
NA_W = NA_HEADS * NA_DH
NA_KH = 8
NA_KW = 16
NA_UNROLL = 8
HG_HEADS = 32
HG_DK = 128
HG_W = HG_HEADS * HG_DK
HG_CHUNK = 128
HG_SUB = SUBLANE
HG_ROWS = 1024
HG_UNROLL = 8
N_EXPERTS = 64
TOP_K = 8
N_GROUPS = 8
TOPK_GROUPS = 4
E_PER_GROUP = N_EXPERTS // N_GROUPS
D_EXPERT = 384
ROUTED_SCALE = 2.5
MOE_BLOCK = 512
EXPERT_SPLIT = 2
ROUTER_TILE = 512
DISPATCH_TILE = 256
COMBINE_TILE = 128
ISSUE_UNROLL = 8
DEEPNORM_ALPHA = (2.0 * DEPTH) ** 0.25
LN_EPS = 1e-5
NORM_EPS = 1e-6
MASK_NEG = -1e30
LOG2_E = 1.4426950408889634


def _params(*sem):
    return pltpu.CompilerParams(dimension_semantics=sem, vmem_limit_bytes=VMEM_LIMIT)


def _sigmoid_pair(z):
    e = jnp.exp(-jnp.abs(z))
    r = 1.0 / (1.0 + e)
    er = e * r
    pos = z >= 0
    return jnp.where(pos, r, er), jnp.where(pos, er, r)


def _silu(z):
    return z * (0.5 * jnp.tanh(0.5 * z) + 0.5)


def _pack_rows(y):
    half = y.shape[1] // 2
    lo = lax.bitcast_convert_type(y[:, :half].astype(BF16).astype(F32), U32) >> 16
    hi = lax.bitcast_convert_type(y[:, half:].astype(BF16).astype(F32), U32) & jnp.uint32(0xFFFF0000)
    return hi | lo


def _unpack_rows(u):
    lo = lax.bitcast_convert_type(u << 16, F32)
    hi = lax.bitcast_convert_type(u & jnp.uint32(0xFFFF0000), F32)
    return lo, hi


def _mm_kernel(a_ref, w_ref, o_ref, wb_ref, *, heads):
    @pl.when(pl.program_id(1) == 0)
    def _():
        wb_ref[...] = w_ref[...].astype(BF16)

    acc = jnp.dot(a_ref[...], wb_ref[...], preferred_element_type=F32)
    if heads:
        for j in range(o_ref.shape[0]):
            o_ref[j] = acc[:, j * LANE:(j + 1) * LANE].astype(o_ref.dtype)
    else:
        o_ref[...] = acc.astype(o_ref.dtype)


def matmul(a, w, layer, *, out_dtype, heads, tm=1024, tn=512):
    n, k = a.shape
    m = w.shape[2]
    tm, tn = min(tm, n), min(tn, m)
    if heads:
        out_shape = jax.ShapeDtypeStruct((m // LANE, n, LANE), out_dtype)
        out_spec = pl.BlockSpec((tn // LANE, tm, LANE), lambda j, i: (j, i, 0))
    else:
        out_shape = jax.ShapeDtypeStruct((n, m), out_dtype)
        out_spec = pl.BlockSpec((tm, tn), lambda j, i: (i, j))
    return pl.pallas_call(
        functools.partial(_mm_kernel, heads=heads),
        grid=(m // tn, n // tm),
        in_specs=[pl.BlockSpec((tm, k), lambda j, i: (i, 0)),
                  pl.BlockSpec((None, k, tn), lambda j, i: (layer, 0, j))],
        out_specs=out_spec,
        out_shape=out_shape,
        scratch_shapes=[pltpu.VMEM((k, tn), BF16)],
        compiler_params=_params("arbitrary", "arbitrary"),
        name="proj_matmul",
    )(a, w)


def _layer_norm_rows(z, g, b):
    mu = jnp.mean(z, axis=-1, keepdims=True)
    zc = z - mu
    var = jnp.mean(zc * zc, axis=-1, keepdims=True)
    return zc * lax.rsqrt(var + LN_EPS) * g + b


def _ln_kernel(x_ref, m_ref, g_ref, b_ref, y_ref, yb_ref, yp_ref):
    z = DEEPNORM_ALPHA * x_ref[...] + m_ref[...].astype(F32)
    y = _layer_norm_rows(z, g_ref[...], b_ref[...])
    y_ref[...] = y
    yb_ref[...] = y.astype(BF16)
    yp_ref[...] = _pack_rows(y)


def ln_residual(x, m, g, b, *, tm=256):
    n, d = x.shape
    tm = min(tm, n)
    row = pl.BlockSpec((tm, d), lambda i: (i, 0))
    prow = pl.BlockSpec((tm, d // 2), lambda i: (i, 0))
    vec = pl.BlockSpec((1, d), lambda i: (0, 0))
    return pl.pallas_call(
        _ln_kernel,
        grid=(n // tm,),
        in_specs=[row, row, vec, vec],
        out_specs=[row, row, prow],
        out_shape=[jax.ShapeDtypeStruct((n, d), F32), jax.ShapeDtypeStruct((n, d), BF16),
                   jax.ShapeDtypeStruct((n, d // 2), U32)],
        compiler_params=_params("arbitrary"),
        name="ln_residual",
    )(x, m, g.reshape(1, d), b.reshape(1, d))


def _ret_kernel(lg_ref, q_ref, k_ref, v_ref, cos_ref, sin_ref, *rest, rev, n_chunks):
    if rev:
        oa_ref, g_ref, gn_ref, out_ref, s_ref = rest
    else:
        out_ref, s_ref = rest
    C = RET_CHUNK
    h = pl.program_id(1)

    @pl.when(pl.program_id(2) == 0)
    def _():
        s_ref[...] = jnp.zeros_like(s_ref)

    lg = lg_ref[1 if rev else 0, h]
    row = lax.broadcasted_iota(I32, (C, RET_DK), 0).astype(F32)
    if rev:
        q_dec = jnp.exp(lg * (C - row))
        k_dec = jnp.exp(lg * row)
    else:
        q_dec = jnp.exp(lg * (row + 1.0))
        k_dec = jnp.exp(lg * (C - 1.0 - row))
        dist = (lax.broadcasted_iota(I32, (C, C), 0)
                - lax.broadcasted_iota(I32, (C, C), 1)).astype(F32)
        decay = jnp.where(dist >= 0,
                          jnp.exp(lg_ref[0, h] * jnp.maximum(dist, 0.0)),
                          jnp.exp(lg_ref[1, h] * jnp.maximum(-dist, 0.0)))
    chunk_decay = jnp.exp(jnp.full((1, 1), C, F32) * lg)

    def rot(ref, sl, cos, sin):
        t1 = ref[0, sl, :].astype(F32)
        t2 = ref[1, sl, :].astype(F32)
        return jnp.concatenate([t1 * cos - t2 * sin, t1 * sin + t2 * cos], axis=1)

    def local_part(sl):
        cos, sin = cos_ref[sl, :], sin_ref[sl, :]
        qr = rot(q_ref, sl, cos, sin)
        kr = rot(k_ref, sl, cos, sin) * (RET_DK ** -0.5)
        v = jnp.concatenate([v_ref[0, sl, :], v_ref[1, sl, :]], axis=1)
        kv = lax.dot_general((kr * k_dec).astype(BF16), v, (((0,), (0,)), ((), ())),
                             preferred_element_type=F32)
        if rev:
            o = jnp.concatenate([oa_ref[0, sl, :], oa_ref[1, sl, :]], axis=1)
        else:
            scores = lax.dot_general(qr.astype(BF16), kr.astype(BF16), (((1,), (1,)), ((), ())),
                                     preferred_element_type=F32) * decay
            o = jnp.dot(scores.astype(BF16), v, preferred_element_type=F32)
        return o, (qr * q_dec).astype(BF16), kv

    def carried_part(sl, local):
        o, q_in, kv = local
        state = s_ref[...]
        o = o + jnp.dot(q_in, state.astype(BF16), preferred_element_type=F32)
        s_ref[...] = chunk_decay * state + kv
        if rev:
            mu = jnp.mean(o, axis=-1, keepdims=True)
            oc = o - mu
            var = jnp.mean(oc * oc, axis=-1, keepdims=True)
            gate = jnp.concatenate([g_ref[0, sl, :], g_ref[1, sl, :]], axis=1).astype(F32)
            res = oc * lax.rsqrt(var + NORM_EPS) * gn_ref[...] * _silu(gate)
            out_ref[sl, :] = res.astype(out_ref.dtype)
        else:
            out_ref[0, sl, :] = o[:, :LANE]
            out_ref[1, sl, :] = o[:, LANE:]

    order = range(n_chunks - 1, -1, -1) if rev else range(n_chunks)
    slices = [slice(ci * C, (ci + 1) * C) for ci in order]
    locals_ = [local_part(sl) for sl in slices]
    for sl, local in zip(slices, locals_):
        carried_part(sl, local)


def _ret_pass(lg, ph, cos, sin, extra, *, rev, batch, seq, rows):
    n_blk = seq // rows
    n_chunks = rows // RET_CHUNK
    H = RET_HEADS

    def blk(step):
        return (n_blk - 1 - step) if rev else step

    def slab(group):
        return pl.BlockSpec((2, rows, LANE), lambda b, h, s: (group * H + h, b * n_blk + blk(s), 0))

    tab = pl.BlockSpec((rows, LANE), lambda b, h, s: (blk(s), 0))
    in_specs = [pl.BlockSpec(memory_space=pltpu.SMEM), slab(0), slab(1), slab(2), tab, tab]
    args = [lg, ph, ph, ph, cos, sin]
    if rev:
        oa, gn_w = extra
        in_specs += [pl.BlockSpec((2, rows, LANE), lambda b, h, s: (h, b * n_blk + blk(s), 0)),
                     slab(3),
                     pl.BlockSpec((1, RET_DK), lambda b, h, s: (0, h))]
        args += [oa, ph, gn_w.reshape(1, RET_W)]
        out_shape = jax.ShapeDtypeStruct((batch * seq, RET_W), BF16)
        out_spec = pl.BlockSpec((rows, RET_DK), lambda b, h, s: (b * n_blk + blk(s), h))
    else:
        out_shape = jax.ShapeDtypeStruct((2 * H, batch * seq, LANE), F32)
        out_spec = pl.BlockSpec((2, rows, LANE), lambda b, h, s: (h, b * n_blk + blk(s), 0))
    return pl.pallas_call(
        functools.partial(_ret_kernel, rev=rev, n_chunks=n_chunks),
        grid=(batch, H, n_blk),
        in_specs=in_specs,
        out_specs=out_spec,
        out_shape=out_shape,
        scratch_shapes=[pltpu.VMEM((RET_DK, RET_DK), F32)],
        compiler_params=_params("arbitrary", "arbitrary", "arbitrary"),
        name="retention_bwd" if rev else "retention_fwd",
    )(*args)


def retention(ph, decay_exp, gn_w, *, batch, seq, rows=512):
    rows = min(rows, seq)
    log_g = jnp.log1p(-jnp.exp2(-decay_exp.astype(F32)))
    inv = ROPE_BASE ** (-jnp.linspace(0.0, 1.0, RET_DK // 2, dtype=F32))
    ang = jnp.arange(seq, dtype=F32)[:, None] * inv[None, :]
    cos, sin = jnp.cos(ang), jnp.sin(ang)
    o_a = _ret_pass(log_g, ph, cos, sin, None, rev=False, batch=batch, seq=seq, rows=rows)
    return _ret_pass(log_g, ph, cos, sin, (o_a, gn_w), rev=True, batch=batch, seq=seq, rows=rows)


def _na_kernel(q_ref, k_ref, v_ref, tab_ref, o_ref, *, grid_rows):
    span = NA_KH * GRID_W
    scale = NA_DH ** -0.5

    def body(it, carry):
        rows = [it * NA_UNROLL + j for j in range(NA_UNROLL)]
        starts = [jnp.clip(r - NA_KH // 2, 0, grid_rows - NA_KH) for r in rows]
        scores = []
        for r, r0 in zip(rows, starts):
            q = q_ref[0, pl.ds(pl.multiple_of(r * GRID_W, GRID_W), GRID_W), :]
            kw = k_ref[0, pl.ds(pl.multiple_of(r0 * GRID_W, GRID_W), span), :]
            s = lax.dot_general(q, kw, (((1,), (1,)), ((), ())), preferred_element_type=F32)
            scores.append(s * scale + tab_ref[0, r0 - r + NA_KH - 1])
        probs = []
        for s in scores:
            p = jnp.exp(s - jnp.max(s, axis=-1, keepdims=True))
            probs.append((p.astype(BF16), jnp.sum(p, axis=-1, keepdims=True)))
        for r, r0, (p, l) in zip(rows, starts, probs):
            vw = v_ref[0, pl.ds(pl.multiple_of(r0 * GRID_W, GRID_W), span), :]
            o = jnp.dot(p, vw, preferred_element_type=F32) / l
            o_ref[pl.ds(pl.multiple_of(r * GRID_W, GRID_W), GRID_W), :] = o.astype(o_ref.dtype)
        return carry

    lax.fori_loop(0, grid_rows // NA_UNROLL, body, 0)


def _na_bias_table(rpb):
    qc = np.arange(GRID_W)[:, None]
    kc = np.arange(GRID_W)[None, :]
    wstart = np.clip(qc - NA_KW // 2, 0, GRID_W - NA_KW)
    valid = (kc >= wstart) & (kc < wstart + NA_KW)
    dc = np.clip(kc - qc, -(NA_KW - 1), NA_KW - 1) + NA_KW - 1
    onehot = ((dc[None] == np.arange(2 * NA_KW - 1)[:, None, None]) & valid[None]).astype(np.float32)
    by_col = jnp.einsum('hrj,jck->hrck', rpb, jnp.asarray(onehot), precision=lax.Precision.HIGHEST)
    by_col = by_col + jnp.asarray(np.where(valid, 0.0, MASK_NEG).astype(np.float32))[None, None]
    tab = jnp.stack([by_col[:, off:off + NA_KH] for off in range(NA_KH)], axis=1)
    return tab.transpose(0, 1, 3, 2, 4).reshape(NA_HEADS, NA_KH, GRID_W, NA_KH * GRID_W)


def neighborhood_attention(ph, rpb, *, batch, seq, slab0):
    grid_rows = seq // GRID_W
    assert grid_rows >= NA_KH and grid_rows % NA_UNROLL == 0
    tab = _na_bias_table(rpb.astype(F32))

    def slab(group):
        return pl.BlockSpec((1, seq, LANE), lambda b, h: (slab0 + group * NA_HEADS + h, b, 0))

    return pl.pallas_call(
        functools.partial(_na_kernel, grid_rows=grid_rows),
        grid=(batch, NA_HEADS),
        in_specs=[slab(0), slab(1), slab(2),
                  pl.BlockSpec((1, NA_KH, GRID_W, NA_KH * GRID_W), lambda b, h: (h, 0, 0, 0))],
        out_specs=pl.BlockSpec((seq, NA_DH), lambda b, h: (b, h)),
        out_shape=jax.ShapeDtypeStruct((batch * seq, NA_W), BF16),
        compiler_params=_params("arbitrary", "arbitrary"),
        name="neighborhood_attention",
    )(ph, ph, ph, tab)


def _hg_widths():
    widths, w = [], HG_CHUNK // 2
    while w >= HG_SUB:
        widths.append(w)
        w //= 2
    return widths


def _hg_kernel(lb_ref, zq_ref, zf_ref, zi_ref, *rest, rev, layer, n_chunks):
    if rev:
        of_ref, zg_ref, nw_ref, out_ref, st_ref, lvl_ref, dg_ref, bq_ref = rest
    else:
        out_ref, st_ref, lvl_ref, dg_ref, bq_ref = rest
    C = HG_CHUNK
    U = HG_SUB
    widths = _hg_widths()

    @pl.when(pl.program_id(2) == 0)
    def _():
        st_ref[...] = jnp.zeros_like(st_ref)
        s_i = lax.broadcasted_iota(I32, (C, C), 0)
        t_i = lax.broadcasted_iota(I32, (C, C), 1)
        earlier = (s_i > t_i) if rev else (s_i < t_i)
        split = jnp.where(earlier, s_i ^ t_i, 0)
        for l, w in enumerate(widths):
            lvl_ref[l] = jnp.where(split >= w, jnp.where(split < 2 * w, 1.0, 0.0), 0.0)
        near = jnp.where((s_i >= t_i) if rev else (s_i <= t_i), jnp.where((s_i ^ t_i) < U, 1.0, 0.0), 0.0)
        for tt in range(U):
            dg_ref[tt] = near * jnp.where((t_i & (U - 1)) == tt, 1.0, 0.0)

    lbp = lb_ref[...]
    pe = jnp.exp(lbp - jnp.max(lbp, axis=0, keepdims=True))
    share = pe / jnp.sum(pe, axis=0, keepdims=True)
    lb = jnp.sum(share[:layer + 1], axis=0, keepdims=True) - share[0:1]
    one_m_lb = 1.0 - lb

    ri = lax.broadcasted_iota(I32, (C, C), 0)
    ci_ = lax.broadcasted_iota(I32, (C, C), 1)
    tri = jnp.where((ri <= ci_) if rev else (ri >= ci_), 1.0, 0.0).astype(BF16)
    row = lax.broadcasted_iota(I32, (C, HG_DK), 0)

    def rows_of(step):
        cidx = (n_chunks - 1 - step) if rev else step
        return pl.ds(pl.multiple_of(cidx * C, C), C)

    def local_part(sl, slot):
        q = _silu(zq_ref[0, sl, :].astype(F32))
        sig, nsig = _sigmoid_pair(zf_ref[0, sl, :].astype(F32))
        k = one_m_lb * nsig
        lf = jnp.log(lb + one_m_lb * sig) * LOG2_E
        v = zi_ref[0, sl, :]

        hi = lf.astype(BF16)
        r1 = lf - hi.astype(F32)
        mid = r1.astype(BF16)
        lo = (r1 - mid.astype(F32)).astype(BF16)
        parts = jnp.dot(tri, jnp.concatenate([hi, mid, lo], axis=1), preferred_element_type=F32)
        b = parts[:, :LANE] + parts[:, LANE:2 * LANE] + parts[:, 2 * LANE:]

        att_t = jnp.zeros((C, C), F32)
        for l, w in enumerate(widths):
            b3 = b.reshape(C // (2 * w), 2 * w, HG_DK)
            edge = b3[:, w:w + 1, :] if rev else b3[:, w - 1:w, :]
            d = (b3 - edge).reshape(C, HG_DK)
            is_q = ((row & w) == 0) if rev else ((row & w) != 0)
            x = (jnp.where(is_q, q, k) * jnp.exp2(jnp.where(is_q, d, -d))).astype(BF16)
            a = lax.dot_general(x, x, (((1,), (1,)), ((), ())), preferred_element_type=F32)
            att_t = att_t + a * lvl_ref[l]

        bq_ref[slot, 0] = b
        bq_ref[slot, 1] = q

        def step_rows(which, tt):
            return jnp.concatenate(
                [jnp.broadcast_to(bq_ref[slot, which, pl.ds(n * U + tt, 1), :], (U, HG_DK))
                 for n in range(C // U)], axis=0)

        for tt in range(U):
            e = jnp.exp2(jnp.minimum(step_rows(0, tt) - b, 0.0))
            a = jnp.sum(e * k * step_rows(1, tt), axis=-1, keepdims=True)
            att_t = att_t + a * dg_ref[tt]
        o = lax.dot_general(att_t.astype(BF16), v, (((0,), (0,)), ((), ())), preferred_element_type=F32)

        b_end = b[0:1, :] if rev else b[C - 1:C, :]
        kd = (k * jnp.exp2(b_end - b)).astype(BF16)
        kv = lax.dot_general(v, kd, (((0,), (0,)), ((), ())), preferred_element_type=F32)
        return o, (q * jnp.exp2(b)).astype(BF16), kv, jnp.exp2(b_end)

    def carried_part(sl, local):
        o, q_dec, kv, end_decay = local
        state_t = st_ref[...]
        o = o + lax.dot_general(q_dec, state_t.astype(BF16), (((1,), (1,)), ((), ())),
                                preferred_element_type=F32)
        st_ref[...] = state_t * end_decay + kv
        if rev:
            o = o + of_ref[0, sl, :]
            o = o * lax.rsqrt(jnp.mean(o * o, axis=-1, keepdims=True) + NORM_EPS)
            res = o * nw_ref[...] * _silu(zg_ref[0, sl, :].astype(F32))
            out_ref[sl, :] = res.astype(out_ref.dtype)
        else:
            out_ref[0, sl, :] = o

    def group(it, carry):
        slices = [rows_of(it * HG_UNROLL + j) for j in range(HG_UNROLL)]
        locals_ = [local_part(sl, slot) for slot, sl in enumerate(slices)]
        for sl, local in zip(slices, locals_):
            carried_part(sl, local)
        return carry

    lax.fori_loop(0, n_chunks // HG_UNROLL, group, 0)


def _hg_pass(od_lb, ph, extra, *, rev, layer, batch, seq, rows):
    n_blk = seq // rows
    H = HG_HEADS

    def blk(step):
        return (n_blk - 1 - step) if rev else step

    def slab(group):
        return pl.BlockSpec((1, rows, LANE), lambda b, h, s: (group * H + h, b * n_blk + blk(s), 0))

    in_specs = [pl.BlockSpec((od_lb.shape[0], HG_DK), lambda b, h, s: (0, h)),
                slab(0), slab(2 if rev else 1), slab(3)]
    args = [od_lb, ph, ph, ph]
    if rev:
        o_f, norm_w = extra
        in_specs += [pl.BlockSpec((1, rows, LANE), lambda b, h, s: (h, b * n_blk + blk(s), 0)),
                     slab(4),
                     pl.BlockSpec((1, HG_DK), lambda b, h, s: (0, h))]
        args += [o_f, ph, norm_w.reshape(1, HG_W)]
        out_shape = jax.ShapeDtypeStruct((batch * seq, HG_W), BF16)
        out_spec = pl.BlockSpec((rows, HG_DK), lambda b, h, s: (b * n_blk + blk(s), h))
    else:
        out_shape = jax.ShapeDtypeStruct((H, batch * seq, LANE), F32)
        out_spec = pl.BlockSpec((1, rows, LANE), lambda b, h, s: (h, b * n_blk + blk(s), 0))
    return pl.pallas_call(
        functools.partial(_hg_kernel, rev=rev, layer=layer, n_chunks=rows // HG_CHUNK),
        grid=(batch, H, n_blk),
        in_specs=in_specs,
        out_specs=out_spec,
        out_shape=out_shape,
        scratch_shapes=[pltpu.VMEM((HG_DK, HG_DK), F32),
                        pltpu.VMEM((len(_hg_widths()), HG_CHUNK, HG_CHUNK), F32),
                        pltpu.VMEM((HG_SUB, HG_CHUNK, HG_CHUNK), F32),
                        pltpu.VMEM((HG_UNROLL, 2, HG_CHUNK, HG_DK), F32)],
        compiler_params=_params("arbitrary", "arbitrary", "arbitrary"),
        name="hgrn2_bwd" if rev else "hgrn2_fwd",
    )(*args)


def hgrn2(ph, od_lb, norm_w, *, layer, batch, seq):
    rows = min(HG_ROWS, seq)
    od_lb = od_lb.astype(F32)
    o_f = _hg_pass(od_lb, ph, None, rev=False, layer=layer, batch=batch, seq=seq, rows=rows)
    return _hg_pass(od_lb, ph, (o_f, norm_w), rev=True, layer=layer, batch=batch, seq=seq, rows=rows)


def _router_kernel(wt_ref, bias_ref, tri_ref, x_ref, idx_ref, rank_ref, gw_ref, cnt_ref, run_ref):
    tm = x_ref.shape[0]
    G, P = N_GROUPS, E_PER_GROUP

    @pl.when(pl.program_id(0) == 0)
    def _():
        run_ref[...] = jnp.zeros_like(run_ref)

    def split(a):
        hi = a.astype(BF16)
        return hi, (a - hi.astype(F32)).astype(BF16)

    def nt_dot(a, b):
        return lax.dot_general(a, b, (((1,), (1,)), ((), ())), preferred_element_type=F32)

    w_hi, w_lo = split(wt_ref[...])
    x_hi, x_lo = split(x_ref[...])
    logits = nt_dot(w_hi, x_hi) + (nt_dot(w_lo, x_hi) + nt_dot(w_hi, x_lo))
    scores = _sigmoid_pair(logits)[0]
    sc3 = scores.reshape(G, P, tm)
    x3 = (scores + bias_ref[...]).reshape(G, P, tm)
    neg = -jnp.inf

    def max01(a):
        return jnp.max(jnp.max(a, axis=0, keepdims=True), axis=1, keepdims=True)

    def min01(a):
        return jnp.min(jnp.min(a, axis=0, keepdims=True), axis=1, keepdims=True)

    def sum01(a):
        return jnp.sum(jnp.sum(a, axis=0, keepdims=True), axis=1, keepdims=True)

    pidx = lax.broadcasted_iota(I32, (G, P, tm), 1)
    m1 = jnp.max(x3, axis=1, keepdims=True)
    i1 = jnp.min(jnp.where(x3 == m1, pidx, P), axis=1, keepdims=True)
    m2 = jnp.max(jnp.where(pidx == i1, neg, x3), axis=1, keepdims=True)
    gs = m1 + m2
    gidx = lax.broadcasted_iota(I32, (G, 1, tm), 0)
    before = jnp.zeros((G, 1, tm), I32)
    for g in range(G):
        other = gs[g:g + 1]
        ahead = jnp.where(other > gs, 1, jnp.where(other == gs, jnp.where(gidx > g, 1, 0), 0))
        before = before + ahead
    xm = jnp.where(before < TOPK_GROUPS, x3, neg)

    eidx = lax.broadcasted_iota(I32, (G, P, tm), 0) * P + pidx
    picks = []
    sel = jnp.zeros((G, P, tm), F32)
    for _ in range(TOP_K):
        m = max01(xm)
        ik = min01(jnp.where(xm == m, eidx, N_EXPERTS))
        hit = eidx == ik
        xm = jnp.where(hit, neg, xm)
        sel = jnp.where(hit, 1.0, sel)
        picks.append(ik)

    sel2 = sel.reshape(N_EXPERTS, tm)
    prefix = jnp.dot(sel2.astype(BF16), tri_ref[...], preferred_element_type=F32)
    rank3 = (run_ref[...] + prefix).reshape(G, P, tm)
    run_ref[...] = run_ref[...] + jnp.sum(sel2, axis=1, keepdims=True)
    cnt_ref[...] = run_ref[...].astype(I32)

    krow = lax.broadcasted_iota(I32, (TOP_K, tm), 0)
    idx_o = jnp.zeros((TOP_K, tm), I32)
    rank_o = jnp.zeros((TOP_K, tm), F32)
    gw_o = jnp.zeros((TOP_K, tm), F32)
    for kk, ik in enumerate(picks):
        hit = eidx == ik
        rk = sum01(jnp.where(hit, rank3, 0.0)).reshape(1, tm)
        sk = sum01(jnp.where(hit, sc3, 0.0)).reshape(1, tm)
        idx_o = jnp.where(krow == kk, ik.reshape(1, tm), idx_o)
        rank_o = jnp.where(krow == kk, rk, rank_o)
        gw_o = jnp.where(krow == kk, sk, gw_o)
    gw_o = gw_o / jnp.sum(gw_o, axis=0, keepdims=True) * ROUTED_SCALE
    idx_ref[...] = idx_o
    rank_ref[...] = rank_o.astype(I32)
    gw_ref[...] = gw_o


def router(x, w_router, r_bias):
    n, d = x.shape
    tm = min(ROUTER_TILE, n)
    tri = jnp.triu(jnp.ones((tm, tm), BF16), k=1)
    kspec = pl.BlockSpec((TOP_K, tm), lambda i: (0, i))
    return pl.pallas_call(
        _router_kernel,
        grid=(n // tm,),
        in_specs=[pl.BlockSpec((N_EXPERTS, d), lambda i: (0, 0)),
                  pl.BlockSpec((N_EXPERTS, 1), lambda i: (0, 0)),
                  pl.BlockSpec((tm, tm), lambda i: (0, 0)),
                  pl.BlockSpec((tm, d), lambda i: (i, 0))],
        out_specs=[kspec, kspec, kspec, pl.BlockSpec((N_EXPERTS, 1), lambda i: (0, 0))],
        out_shape=[jax.ShapeDtypeStruct((TOP_K, n), I32), jax.ShapeDtypeStruct((TOP_K, n), I32),
                   jax.ShapeDtypeStruct((TOP_K, n), F32), jax.ShapeDtypeStruct((N_EXPERTS, 1), I32)],
        scratch_shapes=[pltpu.VMEM((N_EXPERTS, 1), F32)],
        compiler_params=_params("arbitrary"),
        name="router",
    )(w_router.T, r_bias.astype(F32).reshape(N_EXPERTS, 1), tri, x)


def _dispatch_kernel(dest_ref, pend_ref, pcnt_ref, xp_ref, xg_ref, zero_ref, sem, *, n_tok):
    i = pl.program_id(0)
    tm = xp_ref.shape[0] * SUBLANE

    def zero_block(start):
        return pltpu.make_async_copy(zero_ref, xg_ref.at[pl.ds(pl.multiple_of(start, MOE_BLOCK), MOE_BLOCK)], sem)

    @pl.when(i == 0)
    def _():
        zero_ref[...] = jnp.zeros_like(zero_ref)
        n_blocks = xg_ref.shape[0] // MOE_BLOCK
        n_used = pend_ref[N_EXPERTS - 1] // MOE_BLOCK

        def start(e, c):
            @pl.when(pcnt_ref[e] > 0)
            def _():
                zero_block(pend_ref[e] - MOE_BLOCK).start()
            return c

        def wait(e, c):
            @pl.when(pcnt_ref[e] > 0)
            def _():
                zero_block(pend_ref[e] - MOE_BLOCK).wait()
            return c

        def start_tail(blk, c):
            zero_block(blk * MOE_BLOCK).start()
            return c

        def wait_tail(blk, c):
            zero_block(blk * MOE_BLOCK).wait()
            return c

        lax.fori_loop(0, N_EXPERTS, start, 0)
        lax.fori_loop(n_used, n_blocks, start_tail, 0)
        lax.fori_loop(0, N_EXPERTS, wait, 0)
        lax.fori_loop(n_used, n_blocks, wait_tail, 0)

    def issue(grp, c):
        for s in range(SUBLANE):
            for k in range(TOP_K):
                d = dest_ref[k * n_tok + i * tm + grp * SUBLANE + s]
                pltpu.make_async_copy(xp_ref.at[grp, pl.ds(s, 1)], xg_ref.at[pl.ds(d, 1)], sem).start()
        return c

    lax.fori_loop(0, tm // SUBLANE, issue, 0)
    for _ in range(TOP_K):
        pltpu.make_async_copy(zero_ref.at[pl.ds(0, tm)], xg_ref.at[pl.ds(0, tm)], sem).wait()


def dispatch(xp, dest, pends, pcounts, *, n_rows):
    n, half = xp.shape
    tm = min(DISPATCH_TILE, n)
    assert tm <= MOE_BLOCK
    grid_spec = pltpu.PrefetchScalarGridSpec(
        num_scalar_prefetch=3,
        grid=(n // tm,),
        in_specs=[pl.BlockSpec((tm // SUBLANE, SUBLANE, half), lambda i, *_: (i, 0, 0))],
        out_specs=pl.BlockSpec(memory_space=pl.ANY),
        scratch_shapes=[pltpu.VMEM((MOE_BLOCK, half), U32), pltpu.SemaphoreType.DMA],
    )
    return pl.pallas_call(
        functools.partial(_dispatch_kernel, n_tok=n),
        grid_spec=grid_spec,
        out_shape=jax.ShapeDtypeStruct((n_rows, half), U32),
        compiler_params=_params("arbitrary"),
        name="moe_dispatch",
    )(dest, pends, pcounts, xp.reshape(n // SUBLANE, SUBLANE, half))


def _expert_kernel(be_ref, nx_ref, nu_ref, x_ref, wg_hbm, wu_hbm, wd_hbm, y_ref,
                   wg_st, wu_st, wd_st, wgu_s, wd_s, sem, *, layer):
    i = pl.program_id(0)
    used = i < nu_ref[0]
    expert = be_ref[i]
    first = jnp.logical_or(i == 0, expert != be_ref[jnp.maximum(i - 1, 0)])

    def fetch(e):
        return (pltpu.make_async_copy(wg_hbm.at[layer, e], wg_st, sem.at[0]),
                pltpu.make_async_copy(wu_hbm.at[layer, e], wu_st, sem.at[1]),
                pltpu.make_async_copy(wd_hbm.at[layer, e], wd_st, sem.at[2]))

    @pl.when(i == 0)
    def _():
        for c in fetch(expert):
            c.start()

    @pl.when(jnp.logical_and(used, first))
    def _():
        for c in fetch(expert):
            c.wait()
        wgu_s[:, :D_EXPERT] = wg_st[...].astype(BF16)
        wgu_s[:, D_EXPERT:] = wu_st[...].astype(BF16)
        wd_s[...] = wd_st[...].astype(BF16)
        nxt = nx_ref[i]

        @pl.when(nxt >= 0)
        def _():
            for c in fetch(nxt):
                c.start()

    @pl.when(used)
    def _():
        sub = MOE_BLOCK // EXPERT_SPLIT
        hs = []
        for r in range(EXPERT_SPLIT):
            lo, hi = _unpack_rows(x_ref[r * sub:(r + 1) * sub, :])
            x = jnp.concatenate([lo, hi], axis=1).astype(BF16)
            hs.append(jnp.dot(x, wgu_s[...], preferred_element_type=F32))
        for r, h in enumerate(hs):
            hid = (_silu(h[:, :D_EXPERT]) * h[:, D_EXPERT:]).astype(BF16)
            y_ref[r * sub:(r + 1) * sub, :] = _pack_rows(jnp.dot(hid, wd_s[...], preferred_element_type=F32))

    @pl.when(jnp.logical_not(used))
    def _():
        y_ref[...] = jnp.zeros_like(y_ref)


def expert_blocks(xg, blk_e, blk_next, n_used, w_gate, w_up, w_down, layer):
    n_rows, half = xg.shape
    d = 2 * half
    n_blocks = n_rows // MOE_BLOCK

    def xmap(i, be, nx, nu):
        return (jnp.minimum(i, nu[0] - 1), 0)

    hbm = pl.BlockSpec(memory_space=pl.ANY)
    grid_spec = pltpu.PrefetchScalarGridSpec(
        num_scalar_prefetch=3,
        grid=(n_blocks,),
        in_specs=[pl.BlockSpec((MOE_BLOCK, half), xmap), hbm, hbm, hbm],
        out_specs=pl.BlockSpec((MOE_BLOCK, half), lambda i, be, nx, nu: (i, 0)),
        scratch_shapes=[pltpu.VMEM((d, D_EXPERT), F32), pltpu.VMEM((d, D_EXPERT), F32),
                        pltpu.VMEM((D_EXPERT, d), F32),
                        pltpu.VMEM((d, 2 * D_EXPERT), BF16), pltpu.VMEM((D_EXPERT, d), BF16),
                        pltpu.SemaphoreType.DMA((3,))],
    )
    return pl.pallas_call(
        functools.partial(_expert_kernel, layer=layer),
        grid_spec=grid_spec,
        out_shape=jax.ShapeDtypeStruct((n_rows, half), U32),
        compiler_params=_params("arbitrary"),
        name="expert_blocks",
    )(blk_e, blk_next, n_used, xg, w_gate, w_up, w_down)


def _dest_kernel(ps_ref, idx_ref, rank_ref, dest_ref):
    idx = idx_ref[...]
    dest = rank_ref[...]
    for e in range(N_EXPERTS):
        dest = dest + jnp.where(idx == e, ps_ref[e], 0)
    dest_ref[...] = dest


def dispatch_rows(idx, rank, pstarts, *, tn=2048):
    k, n = idx.shape
    tn = min(tn, n)
    spec = pl.BlockSpec((k, tn), lambda i, ps: (0, i))
    return pl.pallas_call(
        _dest_kernel,
        grid_spec=pltpu.PrefetchScalarGridSpec(num_scalar_prefetch=1, grid=(n // tn,),
                                               in_specs=[spec, spec], out_specs=spec),
        out_shape=jax.ShapeDtypeStruct((k, n), I32),
        compiler_params=_params("arbitrary"),
        name="dispatch_rows",
    )(pstarts, idx, rank)


def _combine_kernel(dest_ref, x_ref, xb_ref, gw_ref, sg_ref, su_ref, sd_ref, g_ref, b_ref, yg_ref,
                    y_ref, yb_ref, yp_ref, buf_ref, sem, *, n_tok):
    i = pl.program_id(0)
    n_tiles = pl.num_programs(0)
    tm = x_ref.shape[0]
    slot = i % 2

    def gather(tile, to_slot):
        def issue(grp, c):
            for s in range(SUBLANE):
                for k in range(TOP_K):
                    d = dest_ref[k * n_tok + tile * tm + grp * SUBLANE + s]
                    pltpu.make_async_copy(yg_ref.at[pl.ds(d, 1)], buf_ref.at[to_slot, k, grp, pl.ds(s, 1)],
                                          sem.at[to_slot]).start()
            return c
        lax.fori_loop(0, tm // SUBLANE, issue, 0)

    @pl.when(i == 0)
    def _():
        gather(0, 0)

    @pl.when(i + 1 < n_tiles)
    def _():
        gather(i + 1, 1 - slot)

    xb = xb_ref[...]
    hg = jnp.dot(xb, sg_ref[...], preferred_element_type=F32)
    hu = jnp.dot(xb, su_ref[...], preferred_element_type=F32)
    hid = (_silu(hg) * hu).astype(BF16)
    f = jnp.dot(hid, sd_ref[...], preferred_element_type=F32)

    for k in range(TOP_K):
        pltpu.make_async_copy(yg_ref.at[pl.ds(0, tm)], yp_ref, sem.at[slot]).wait()
    gw = gw_ref[...]
    half = x_ref.shape[1] // 2
    lo = jnp.zeros((tm, half), F32)
    hi = jnp.zeros((tm, half), F32)
    for k in range(TOP_K):
        l_k, h_k = _unpack_rows(buf_ref[slot, k].reshape(tm, half))
        lo = lo + gw[:, k:k + 1] * l_k
        hi = hi + gw[:, k:k + 1] * h_k
    f = f + jnp.concatenate([lo, hi], axis=1)
    y = _layer_norm_rows(DEEPNORM_ALPHA * x_ref[...] + f, g_ref[...], b_ref[...])
    y_ref[...] = y
    yb_ref[...] = y.astype(BF16)
    yp_ref[...] = _pack_rows(y)


def moe_combine(x, xb, gw, dest, yg, s_gate, s_up, s_down, g, b):
    n, d = x.shape
    tm = min(COMBINE_TILE, n)
    once = pl.Buffered(1)
    row = pl.BlockSpec((tm, d), lambda i, *_: (i, 0))
    prow = pl.BlockSpec((tm, d // 2), lambda i, *_: (i, 0))
    vec = pl.BlockSpec((1, d), lambda i, *_: (0, 0), pipeline_mode=once)
    grid_spec = pltpu.PrefetchScalarGridSpec(
        num_scalar_prefetch=1,
        grid=(n // tm,),
        in_specs=[row, row, pl.BlockSpec((tm, TOP_K), lambda i, *_: (i, 0)),
                  pl.BlockSpec((d, D_EXPERT), lambda i, *_: (0, 0), pipeline_mode=once),
                  pl.BlockSpec((d, D_EXPERT), lambda i, *_: (0, 0), pipeline_mode=once),
                  pl.BlockSpec((D_EXPERT, d), lambda i, *_: (0, 0), pipeline_mode=once),
                  vec, vec, pl.BlockSpec(memory_space=pl.ANY)],
        out_specs=[row, row, prow],
        scratch_shapes=[pltpu.VMEM((2, TOP_K, tm // SUBLANE, SUBLANE, d // 2), U32),
                        pltpu.SemaphoreType.DMA((2,))],
    )
    return pl.pallas_call(
        functools.partial(_combine_kernel, n_tok=n),
        grid_spec=grid_spec,
        out_shape=[jax.ShapeDtypeStruct((n, d), F32), jax.ShapeDtypeStruct((n, d), BF16),
                   jax.ShapeDtypeStruct((n, d // 2), U32)],
        compiler_params=_params("arbitrary"),
        name="moe_combine",
    )(dest, x, xb, gw, s_gate, s_up, s_down, g.reshape(1, d), b.reshape(1, d), yg)


def moe_ffn(x, xb, xp, layer, w_router, r_bias, w_gate, w_up, w_down, s_gate, s_up, s_down, ln_g, ln_b):
    n_tok = x.shape[0]
    idx, rank, gw, counts = router(x, w_router, r_bias)
    counts = counts.reshape(N_EXPERTS)
    pcounts = (counts + MOE_BLOCK - 1) // MOE_BLOCK * MOE_BLOCK
    pends = jnp.cumsum(pcounts).astype(I32)
    pstarts = pends - pcounts
    n_blocks = n_tok * TOP_K // MOE_BLOCK + N_EXPERTS
    blk_start = jnp.arange(n_blocks, dtype=I32) * MOE_BLOCK
    blk_e = jnp.minimum(jnp.sum(pends[None, :] <= blk_start[:, None], axis=1), N_EXPERTS - 1).astype(I32)
    n_used = (pends[-1:] // MOE_BLOCK).astype(I32)
    eids = jnp.arange(N_EXPERTS, dtype=I32)
    later = jnp.where((eids[None, :] > eids[:, None]) & (pcounts[None, :] > 0), eids[None, :], N_EXPERTS)
    nxt_e = jnp.min(later, axis=1)
    nxt_e = jnp.where(nxt_e == N_EXPERTS, -1, nxt_e)
    blk_next = jnp.sum(jnp.where(blk_e[:, None] == eids[None, :], nxt_e[None, :], 0), axis=1).astype(I32)
    dest = dispatch_rows(idx, rank, pstarts.astype(I32)).reshape(TOP_K * n_tok)

    xg = dispatch(xp, dest, pends, pcounts.astype(I32), n_rows=n_blocks * MOE_BLOCK)
    yg = expert_blocks(xg, blk_e, blk_next, n_used, w_gate, w_up, w_down, layer)
    return moe_combine(x, xb, gw.T, dest, yg, s_gate, s_up, s_down, ln_g, ln_b)


def kernel(x, ln_g, ln_b, ev_w_in, ev_decay_exp, ev_gn_w, ev_rpb, ev_w_out, od_w_in, od_lb, od_norm_w, od_w_out, moe_w_router, moe_bias, moe_w_gate, moe_w_up, moe_w_down, sh_w_gate, sh_w_up, sh_w_down):
    batch, seq, d = x.shape
    n = batch * seq
    xf = x.reshape(n, d).astype(F32)
    xb = xf.astype(BF16)
    for layer in range(DEPTH):
        j = layer // 2
        if layer % 2 == 0:
            ph = matmul(xb, ev_w_in, j, out_dtype=BF16, heads=True)
            o_ret = retention(ph, ev_decay_exp[j], ev_gn_w[j].astype(F32), batch=batch, seq=seq)
            o_na = neighborhood_attention(ph, ev_rpb[j], batch=batch, seq=seq, slab0=4 * RET_W // LANE)
            mixed = jnp.concatenate([o_ret, o_na], axis=1)
            w_out = ev_w_out
        else:
            ph = matmul(xb, od_w_in, j, out_dtype=BF16, heads=True)
            mixed = hgrn2(ph, od_lb, od_norm_w[j].astype(F32), layer=layer, batch=batch, seq=seq)
            w_out = od_w_out
        m = matmul(mixed, w_out, j, out_dtype=F32, heads=False)
        xf, xb, xp = ln_residual(xf, m, ln_g[layer, 0].astype(F32), ln_b[layer, 0].astype(F32))
        xf, xb, xp = moe_ffn(xf, xb, xp, layer, moe_w_router[layer].astype(F32), moe_bias[layer],
                             moe_w_gate, moe_w_up, moe_w_down, sh_w_gate[layer].astype(BF16),
                             sh_w_up[layer].astype(BF16), sh_w_down[layer].astype(BF16),
                             ln_g[layer, 1].astype(F32), ln_b[layer, 1].astype(F32))
    return xf.reshape(batch, seq, d)
```

```python
import functools

import numpy as np
import jax
import jax.numpy as jnp
from jax import lax
from jax.experimental import pallas as pl
from jax.experimental.pallas import tpu as pltpu

F32 = jnp.float32
BF16 = jnp.bfloat16
U32 = jnp.uint32
I32 = jnp.int32

LANE = 128
SUBLANE = 8
VMEM_LIMIT = 56 * 1024 * 1024

DEPTH = 2
GRID_W = 64
RET_HEADS = 8
RET_DK = 256
RET_W = RET_HEADS * RET_DK
RET_CHUNK = 128
ROPE_BASE = 10000.0
NA_HEADS = 16
NA_DH = 128
NA_W = NA_HEADS * NA_DH
NA_KH = 8
NA_KW = 16
NA_UNROLL = 8
HG_HEADS = 32
HG_DK = 128
HG_W = HG_HEADS * HG_DK
HG_CHUNK = 128
HG_SUB = SUBLANE
HG_ROWS = 1024
HG_UNROLL = 8
N_EXPERTS = 64
TOP_K = 8
N_GROUPS = 8
TOPK_GROUPS = 4
E_PER_GROUP = N_EXPERTS // N_GROUPS
D_EXPERT = 384
ROUTED_SCALE = 2.5
MOE_BLOCK = 512
EXPERT_SPLIT = 2
ROUTER_TILE = 512
DISPATCH_TILE = 256
COMBINE_TILE = 128
ISSUE_UNROLL = 8
DEEPNORM_ALPHA = (2.0 * DEPTH) ** 0.25
LN_EPS = 1e-5
NORM_EPS = 1e-6
MASK_NEG = -1e30
LOG2_E = 1.4426950408889634


def _params(*sem):
    return pltpu.CompilerParams(dimension_semantics=sem, vmem_limit_bytes=VMEM_LIMIT)


def _sigmoid_pair(z):
    e = jnp.exp(-jnp.abs(z))
    r = 1.0 / (1.0 + e)
    er = e * r
    pos = z >= 0
    return jnp.where(pos, r, er), jnp.where(pos, er, r)


def _silu(z):
    return z * (0.5 * jnp.tanh(0.5 * z) + 0.5)


def _pack_rows(y):
    half = y.shape[1] // 2
    lo = lax.bitcast_convert_type(y[:, :half].astype(BF16).astype(F32), U32) >> 16
    hi = lax.bitcast_convert_type(y[:, half:].astype(BF16).astype(F32), U32) & jnp.uint32(0xFFFF0000)
    return hi | lo


def _unpack_rows(u):
    lo = lax.bitcast_convert_type(u << 16, F32)
    hi = lax.bitcast_convert_type(u & jnp.uint32(0xFFFF0000), F32)
    return lo, hi


def _mm_kernel(a_ref, w_ref, o_ref, wb_ref, *, heads):
    @pl.when(pl.program_id(1) == 0)
    def _():
        wb_ref[...] = w_ref[...].astype(BF16)

    acc = jnp.dot(a_ref[...], wb_ref[...], preferred_element_type=F32)
    if heads:
        for j in range(o_ref.shape[0]):
            o_ref[j] = acc[:, j * LANE:(j + 1) * LANE].astype(o_ref.dtype)
    else:
        o_ref[...] = acc.astype(o_ref.dtype)


def matmul(a, w, layer, *, out_dtype, heads, tm=1024, tn=512):
    n, k = a.shape
    m = w.shape[2]
    tm, tn = min(tm, n), min(tn, m)
    if heads:
        out_shape = jax.ShapeDtypeStruct((m // LANE, n, LANE), out_dtype)
        out_spec = pl.BlockSpec((tn // LANE, tm, LANE), lambda j, i: (j, i, 0))
    else:
        out_shape = jax.ShapeDtypeStruct((n, m), out_dtype)
        out_spec = pl.BlockSpec((tm, tn), lambda j, i: (i, j))
    return pl.pallas_call(
        functools.partial(_mm_kernel, heads=heads),
        grid=(m // tn, n // tm),
        in_specs=[pl.BlockSpec((tm, k), lambda j, i: (i, 0)),
                  pl.BlockSpec((None, k, tn), lambda j, i: (layer, 0, j))],
        out_specs=out_spec,
        out_shape=out_shape,
        scratch_shapes=[pltpu.VMEM((k, tn), BF16)],
        compiler_params=_params("arbitrary", "arbitrary"),
        name="proj_matmul",
    )(a, w)


def _layer_norm_rows(z, g, b):
    mu = jnp.mean(z, axis=-1, keepdims=True)
    zc = z - mu
    var = jnp.mean(zc * zc, axis=-1, keepdims=True)
    return zc * lax.rsqrt(var + LN_EPS) * g + b


def _ln_kernel(x_ref, m_ref, g_ref, b_ref, y_ref, yb_ref, yp_ref):
    z = DEEPNORM_ALPHA * x_ref[...] + m_ref[...].astype(F32)
    y = _layer_norm_rows(z, g_ref[...], b_ref[...])
    y_ref[...] = y
    yb_ref[...] = y.astype(BF16)
    yp_ref[...] = _pack_rows(y)


def ln_residual(x, m, g, b, *, tm=256):
    n, d = x.shape
    tm = min(tm, n)
    row = pl.BlockSpec((tm, d), lambda i: (i, 0))
    prow = pl.BlockSpec((tm, d // 2), lambda i: (i, 0))
    vec = pl.BlockSpec((1, d), lambda i: (0, 0))
    return pl.pallas_call(
        _ln_kernel,
        grid=(n // tm,),
        in_specs=[row, row, vec, vec],
        out_specs=[row, row, prow],
        out_shape=[jax.ShapeDtypeStruct((n, d), F32), jax.ShapeDtypeStruct((n, d), BF16),
                   jax.ShapeDtypeStruct((n, d // 2), U32)],
        compiler_params=_params("arbitrary"),
        name="ln_residual",
    )(x, m, g.reshape(1, d), b.reshape(1, d))


def _ret_kernel(lg_ref, q_ref, k_ref, v_ref, cos_ref, sin_ref, *rest, rev, n_chunks):
    if rev:
        oa_ref, g_ref, gn_ref, out_ref, s_ref = rest
    else:
        out_ref, s_ref = rest
    C = RET_CHUNK
    h = pl.program_id(1)

    @pl.when(pl.program_id(2) == 0)
    def _():
        s_ref[...] = jnp.zeros_like(s_ref)

    lg = lg_ref[1 if rev else 0, h]
    row = lax.broadcasted_iota(I32, (C, RET_DK), 0).astype(F32)
    if rev:
        q_dec = jnp.exp(lg * (C - row))
        k_dec = jnp.exp(lg * row)
    else:
        q_dec = jnp.exp(lg * (row + 1.0))
        k_dec = jnp.exp(lg * (C - 1.0 - row))
        dist = (lax.broadcasted_iota(I32, (C, C), 0)
                - lax.broadcasted_iota(I32, (C, C), 1)).astype(F32)
        decay = jnp.where(dist >= 0,
                          jnp.exp(lg_ref[0, h] * jnp.maximum(dist, 0.0)),
                          jnp.exp(lg_ref[1, h] * jnp.maximum(-dist, 0.0)))
    chunk_decay = jnp.exp(jnp.full((1, 1), C, F32) * lg)

    def rot(ref, sl, cos, sin):
        t1 = ref[0, sl, :].astype(F32)
        t2 = ref[1, sl, :].astype(F32)
        return jnp.concatenate([t1 * cos - t2 * sin, t1 * sin + t2 * cos], axis=1)

    def local_part(sl):
        cos, sin = cos_ref[sl, :], sin_ref[sl, :]
        qr = rot(q_ref, sl, cos, sin)
        kr = rot(k_ref, sl, cos, sin) * (RET_DK ** -0.5)
        v = jnp.concatenate([v_ref[0, sl, :], v_ref[1, sl, :]], axis=1)
        kv = lax.dot_general((kr * k_dec).astype(BF16), v, (((0,), (0,)), ((), ())),
                             preferred_element_type=F32)
        if rev:
            o = jnp.concatenate([oa_ref[0, sl, :], oa_ref[1, sl, :]], axis=1)
        else:
            scores = lax.dot_general(qr.astype(BF16), kr.astype(BF16), (((1,), (1,)), ((), ())),
                                     preferred_element_type=F32) * decay
            o = jnp.dot(scores.astype(BF16), v, preferred_element_type=F32)
        return o, (qr * q_dec).astype(BF16), kv

    def carried_part(sl, local):
        o, q_in, kv = local
        state = s_ref[...]
        o = o + jnp.dot(q_in, state.astype(BF16), preferred_element_type=F32)
        s_ref[...] = chunk_decay * state + kv
        if rev:
            mu = jnp.mean(o, axis=-1, keepdims=True)
            oc = o - mu
            var = jnp.mean(oc * oc, axis=-1, keepdims=True)
            gate = jnp.concatenate([g_ref[0, sl, :], g_ref[1, sl, :]], axis=1).astype(F32)
            res = oc * lax.rsqrt(var + NORM_EPS) * gn_ref[...] * _silu(gate)
            out_ref[sl, :] = res.astype(out_ref.dtype)
        else:
            out_ref[0, sl, :] = o[:, :LANE]
            out_ref[1, sl, :] = o[:, LANE:]

    order = range(n_chunks - 1, -1, -1) if rev else range(n_chunks)
    slices = [slice(ci * C, (ci + 1) * C) for ci in order]
    locals_ = [local_part(sl) for sl in slices]
    for sl, local in zip(slices, locals_):
        carried_part(sl, local)


def _ret_pass(lg, ph, cos, sin, extra, *, rev, batch, seq, rows):
    n_blk = seq // rows
    n_chunks = rows // RET_CHUNK
    H = RET_HEADS

    def blk(step):
        return (n_blk - 1 - step) if rev else step

    def slab(group):
        return pl.BlockSpec((2, rows, LANE), lambda b, h, s: (group * H + h, b * n_blk + blk(s), 0))

    tab = pl.BlockSpec((rows, LANE), lambda b, h, s: (blk(s), 0))
    in_specs = [pl.BlockSpec(memory_space=pltpu.SMEM), slab(0), slab(1), slab(2), tab, tab]
    args = [lg, ph, ph, ph, cos, sin]
    if rev:
        oa, gn_w = extra
        in_specs += [pl.BlockSpec((2, rows, LANE), lambda b, h, s: (h, b * n_blk + blk(s), 0)),
                     slab(3),
                     pl.BlockSpec((1, RET_DK), lambda b, h, s: (0, h))]
        args += [oa, ph, gn_w.reshape(1, RET_W)]
        out_shape = jax.ShapeDtypeStruct((batch * seq, RET_W), BF16)
        out_spec = pl.BlockSpec((rows, RET_DK), lambda b, h, s: (b * n_blk + blk(s), h))
    else:
        out_shape = jax.ShapeDtypeStruct((2 * H, batch * seq, LANE), F32)
        out_spec = pl.BlockSpec((2, rows, LANE), lambda b, h, s: (h, b * n_blk + blk(s), 0))
    return pl.pallas_call(
        functools.partial(_ret_kernel, rev=rev, n_chunks=n_chunks),
        grid=(batch, H, n_blk),
        in_specs=in_specs,
        out_specs=out_spec,
        out_shape=out_shape,
        scratch_shapes=[pltpu.VMEM((RET_DK, RET_DK), F32)],
        compiler_params=_params("arbitrary", "arbitrary", "arbitrary"),
        name="retention_bwd" if rev else "retention_fwd",
    )(*args)


def retention(ph, decay_exp, gn_w, *, batch, seq, rows=512):
    rows = min(rows, seq)
    log_g = jnp.log1p(-jnp.exp2(-decay_exp.astype(F32)))
    inv = ROPE_BASE ** (-jnp.linspace(0.0, 1.0, RET_DK // 2, dtype=F32))
    ang = jnp.arange(seq, dtype=F32)[:, None] * inv[None, :]
    cos, sin = jnp.cos(ang), jnp.sin(ang)
    o_a = _ret_pass(log_g, ph, cos, sin, None, rev=False, batch=batch, seq=seq, rows=rows)
    return _ret_pass(log_g, ph, cos, sin, (o_a, gn_w), rev=True, batch=batch, seq=seq, rows=rows)


def _na_kernel(q_ref, k_ref, v_ref, tab_ref, o_ref, *, grid_rows):
    span = NA_KH * GRID_W
    scale = NA_DH ** -0.5

    def body(it, carry):
        rows = [it * NA_UNROLL + j for j in range(NA_UNROLL)]
        starts = [jnp.clip(r - NA_KH // 2, 0, grid_rows - NA_KH) for r in rows]
        scores = []
        for r, r0 in zip(rows, starts):
            q = q_ref[0, pl.ds(pl.multiple_of(r * GRID_W, GRID_W), GRID_W), :]
            kw = k_ref[0, pl.ds(pl.multiple_of(r0 * GRID_W, GRID_W), span), :]
            s = lax.dot_general(q, kw, (((1,), (1,)), ((), ())), preferred_element_type=F32)
            scores.append(s * scale + tab_ref[0, r0 - r + NA_KH - 1])
        probs = []
        for s in scores:
            p = jnp.exp(s - jnp.max(s, axis=-1, keepdims=True))
            probs.append((p.astype(BF16), jnp.sum(p, axis=-1, keepdims=True)))
        for r, r0, (p, l) in zip(rows, starts, probs):
            vw = v_ref[0, pl.ds(pl.multiple_of(r0 * GRID_W, GRID_W), span), :]
            o = jnp.dot(p, vw, preferred_element_type=F32) / l
            o_ref[pl.ds(pl.multiple_of(r * GRID_W, GRID_W), GRID_W), :] = o.astype(o_ref.dtype)
        return carry

    lax.fori_loop(0, grid_rows // NA_UNROLL, body, 0)


def _na_bias_table(rpb):
    qc = np.arange(GRID_W)[:, None]
    kc = np.arange(GRID_W)[None, :]
    wstart = np.clip(qc - NA_KW // 2, 0, GRID_W - NA_KW)
    valid = (kc >= wstart) & (kc < wstart + NA_KW)
    dc = np.clip(kc - qc, -(NA_KW - 1), NA_KW - 1) + NA_KW - 1
    onehot = ((dc[None] == np.arange(2 * NA_KW - 1)[:, None, None]) & valid[None]).astype(np.float32)
    by_col = jnp.einsum('hrj,jck->hrck', rpb, jnp.asarray(onehot), precision=lax.Precision.HIGHEST)
    by_col = by_col + jnp.asarray(np.where(valid, 0.0, MASK_NEG).astype(np.float32))[None, None]
    tab = jnp.stack([by_col[:, off:off + NA_KH] for off in range(NA_KH)], axis=1)
    return tab.transpose(0, 1, 3, 2, 4).reshape(NA_HEADS, NA_KH, GRID_W, NA_KH * GRID_W)


def neighborhood_attention(ph, rpb, *, batch, seq, slab0):
    grid_rows = seq // GRID_W
    assert grid_rows >= NA_KH and grid_rows % NA_UNROLL == 0
    tab = _na_bias_table(rpb.astype(F32))

    def slab(group):
        return pl.BlockSpec((1, seq, LANE), lambda b, h: (slab0 + group * NA_HEADS + h, b, 0))

    return pl.pallas_call(
        functools.partial(_na_kernel, grid_rows=grid_rows),
        grid=(batch, NA_HEADS),
        in_specs=[slab(0), slab(1), slab(2),
                  pl.BlockSpec((1, NA_KH, GRID_W, NA_KH * GRID_W), lambda b, h: (h, 0, 0, 0))],
        out_specs=pl.BlockSpec((seq, NA_DH), lambda b, h: (b, h)),
        out_shape=jax.ShapeDtypeStruct((batch * seq, NA_W), BF16),
        compiler_params=_params("arbitrary", "arbitrary"),
        name="neighborhood_attention",
    )(ph, ph, ph, tab)


def _hg_widths():
    widths, w = [], HG_CHUNK // 2
    while w >= HG_SUB:
        widths.append(w)
        w //= 2
    return widths


def _hg_kernel(lb_ref, zq_ref, zf_ref, zi_ref, *rest, rev, layer, n_chunks):
    if rev:
        of_ref, zg_ref, nw_ref, out_ref, st_ref, lvl_ref, dg_ref, bq_ref = rest
    else:
        out_ref, st_ref, lvl_ref, dg_ref, bq_ref = rest
    C = HG_CHUNK
    U = HG_SUB
    widths = _hg_widths()

    @pl.when(pl.program_id(2) == 0)
    def _():
        st_ref[...] = jnp.zeros_like(st_ref)
        s_i = lax.broadcasted_iota(I32, (C, C), 0)
        t_i = lax.broadcasted_iota(I32, (C, C), 1)
        earlier = (s_i > t_i) if rev else (s_i < t_i)
        split = jnp.where(earlier, s_i ^ t_i, 0)
        for l, w in enumerate(widths):
            lvl_ref[l] = jnp.where(split >= w, jnp.where(split < 2 * w, 1.0, 0.0), 0.0)
        near = jnp.where((s_i >= t_i) if rev else (s_i <= t_i), jnp.where((s_i ^ t_i) < U, 1.0, 0.0), 0.0)
        for tt in range(U):
            dg_ref[tt] = near * jnp.where((t_i & (U - 1)) == tt, 1.0, 0.0)

    lbp = lb_ref[...]
    pe = jnp.exp(lbp - jnp.max(lbp, axis=0, keepdims=True))
    share = pe / jnp.sum(pe, axis=0, keepdims=True)
    lb = jnp.sum(share[:layer + 1], axis=0, keepdims=True) - share[0:1]
    one_m_lb = 1.0 - lb

    ri = lax.broadcasted_iota(I32, (C, C), 0)
    ci_ = lax.broadcasted_iota(I32, (C, C), 1)
    tri = jnp.where((ri <= ci_) if rev else (ri >= ci_), 1.0, 0.0).astype(BF16)
    row = lax.broadcasted_iota(I32, (C, HG_DK), 0)

    def rows_of(step):
        cidx = (n_chunks - 1 - step) if rev else step
        return pl.ds(pl.multiple_of(cidx * C, C), C)

    def local_part(sl, slot):
        q = _silu(zq_ref[0, sl, :].astype(F32))
        sig, nsig = _sigmoid_pair(zf_ref[0, sl, :].astype(F32))
        k = one_m_lb * nsig
        lf = jnp.log(lb + one_m_lb * sig) * LOG2_E
        v = zi_ref[0, sl, :]

        hi = lf.astype(BF16)
        r1 = lf - hi.astype(F32)
        mid = r1.astype(BF16)
        lo = (r1 - mid.astype(F32)).astype(BF16)
        parts = jnp.dot(tri, jnp.concatenate([hi, mid, lo], axis=1), preferred_element_type=F32)
        b = parts[:, :LANE] + parts[:, LANE:2 * LANE] + parts[:, 2 * LANE:]

        att_t = jnp.zeros((C, C), F32)
        for l, w in enumerate(widths):
            b3 = b.reshape(C // (2 * w), 2 * w, HG_DK)
            edge = b3[:, w:w + 1, :] if rev else b3[:, w - 1:w, :]
            d = (b3 - edge).reshape(C, HG_DK)
            is_q = ((row & w) == 0) if rev else ((row & w) != 0)
            x = (jnp.where(is_q, q, k) * jnp.exp2(jnp.where(is_q, d, -d))).astype(BF16)
            a = lax.dot_general(x, x, (((1,), (1,)), ((), ())), preferred_element_type=F32)
            att_t = att_t + a * lvl_ref[l]

        bq_ref[slot, 0] = b
        bq_ref[slot, 1] = q

        def step_rows(which, tt):
            return jnp.concatenate(
                [jnp.broadcast_to(bq_ref[slot, which, pl.ds(n * U + tt, 1), :], (U, HG_DK))
                 for n in range(C // U)], axis=0)

        for tt in range(U):
            e = jnp.exp2(jnp.minimum(step_rows(0, tt) - b, 0.0))
            a = jnp.sum(e * k * step_rows(1, tt), axis=-1, keepdims=True)
            att_t = att_t + a * dg_ref[tt]
        o = lax.dot_general(att_t.astype(BF16), v, (((0,), (0,)), ((), ())), preferred_element_type=F32)

        b_end = b[0:1, :] if rev else b[C - 1:C, :]
        kd = (k * jnp.exp2(b_end - b)).astype(BF16)
        kv = lax.dot_general(v, kd, (((0,), (0,)), ((), ())), preferred_element_type=F32)
        return o, (q * jnp.exp2(b)).astype(BF16), kv, jnp.exp2(b_end)

    def carried_part(sl, local):
        o, q_dec, kv, end_decay = local
        state_t = st_ref[...]
        o = o + lax.dot_general(q_dec, state_t.astype(BF16), (((1,), (1,)), ((), ())),
                                preferred_element_type=F32)
        st_ref[...] = state_t * end_decay + kv
        if rev:
            o = o + of_ref[0, sl, :]
            o = o * lax.rsqrt(jnp.mean(o * o, axis=-1, keepdims=True) + NORM_EPS)
            res = o * nw_ref[...] * _silu(zg_ref[0, sl, :].astype(F32))
            out_ref[sl, :] = res.astype(out_ref.dtype)
        else:
            out_ref[0, sl, :] = o

    def group(it, carry):
        slices = [rows_of(it * HG_UNROLL + j) for j in range(HG_UNROLL)]
        locals_ = [local_part(sl, slot) for slot, sl in enumerate(slices)]
        for sl, local in zip(slices, locals_):
            carried_part(sl, local)
        return carry

    lax.fori_loop(0, n_chunks // HG_UNROLL, group, 0)


def _hg_pass(od_lb, ph, extra, *, rev, layer, batch, seq, rows):
    n_blk = seq // rows
    H = HG_HEADS

    def blk(step):
        return (n_blk - 1 - step) if rev else step

    def slab(group):
        return pl.BlockSpec((1, rows, LANE), lambda b, h, s: (group * H + h, b * n_blk + blk(s), 0))

    in_specs = [pl.BlockSpec((od_lb.shape[0], HG_DK), lambda b, h, s: (0, h)),
                slab(0), slab(2 if rev else 1), slab(3)]
    args = [od_lb, ph, ph, ph]
    if rev:
        o_f, norm_w = extra
        in_specs += [pl.BlockSpec((1, rows, LANE), lambda b, h, s: (h, b * n_blk + blk(s), 0)),
                     slab(4),
                     pl.BlockSpec((1, HG_DK), lambda b, h, s: (0, h))]
        args += [o_f, ph, norm_w.reshape(1, HG_W)]
        out_shape = jax.ShapeDtypeStruct((batch * seq, HG_W), BF16)
        out_spec = pl.BlockSpec((rows, HG_DK), lambda b, h, s: (b * n_blk + blk(s), h))
    else:
        out_shape = jax.ShapeDtypeStruct((H, batch * seq, LANE), F32)
        out_spec = pl.BlockSpec((1, rows, LANE), lambda b, h, s: (h, b * n_blk + blk(s), 0))
    return pl.pallas_call(
        functools.partial(_hg_kernel, rev=rev, layer=layer, n_chunks=rows // HG_CHUNK),
        grid=(batch, H, n_blk),
        in_specs=in_specs,
        out_specs=out_spec,
        out_shape=out_shape,
        scratch_shapes=[pltpu.VMEM((HG_DK, HG_DK), F32),
                        pltpu.VMEM((len(_hg_widths()), HG_CHUNK, HG_CHUNK), F32),
                        pltpu.VMEM((HG_SUB, HG_CHUNK, HG_CHUNK), F32),
                        pltpu.VMEM((HG_UNROLL, 2, HG_CHUNK, HG_DK), F32)],
        compiler_params=_params("arbitrary", "arbitrary", "arbitrary"),
        name="hgrn2_bwd" if rev else "hgrn2_fwd",
    )(*args)


def hgrn2(ph, od_lb, norm_w, *, layer, batch, seq):
    rows = min(HG_ROWS, seq)
    od_lb = od_lb.astype(F32)
    o_f = _hg_pass(od_lb, ph, None, rev=False, layer=layer, batch=batch, seq=seq, rows=rows)
    return _hg_pass(od_lb, ph, (o_f, norm_w), rev=True, layer=layer, batch=batch, seq=seq, rows=rows)


def _router_kernel(wt_ref, bias_ref, tri_ref, x_ref, idx_ref, rank_ref, gw_ref, cnt_ref, run_ref):
    tm = x_ref.shape[0]
    G, P = N_GROUPS, E_PER_GROUP

    @pl.when(pl.program_id(0) == 0)
    def _():
        run_ref[...] = jnp.zeros_like(run_ref)

    def split(a):
        hi = a.astype(BF16)
        return hi, (a - hi.astype(F32)).astype(BF16)

    def nt_dot(a, b):
        return lax.dot_general(a, b, (((1,), (1,)), ((), ())), preferred_element_type=F32)

    w_hi, w_lo = split(wt_ref[...])
    x_hi, x_lo = split(x_ref[...])
    logits = nt_dot(w_hi, x_hi) + (nt_dot(w_lo, x_hi) + nt_dot(w_hi, x_lo))
    scores = _sigmoid_pair(logits)[0]
    sc3 = scores.reshape(G, P, tm)
    x3 = (scores + bias_ref[...]).reshape(G, P, tm)
    neg = -jnp.inf

    def max01(a):
        return jnp.max(jnp.max(a, axis=0, keepdims=True), axis=1, keepdims=True)

    def min01(a):
        return jnp.min(jnp.min(a, axis=0, keepdims=True), axis=1, keepdims=True)

    def sum01(a):
        return jnp.sum(jnp.sum(a, axis=0, keepdims=True), axis=1, keepdims=True)

    pidx = lax.broadcasted_iota(I32, (G, P, tm), 1)
    m1 = jnp.max(x3, axis=1, keepdims=True)
    i1 = jnp.min(jnp.where(x3 == m1, pidx, P), axis=1, keepdims=True)
    m2 = jnp.max(jnp.where(pidx == i1, neg, x3), axis=1, keepdims=True)
    gs = m1 + m2
    gidx = lax.broadcasted_iota(I32, (G, 1, tm), 0)
    before = jnp.zeros((G, 1, tm), I32)
    for g in range(G):
        other = gs[g:g + 1]
        ahead = jnp.where(other > gs, 1, jnp.where(other == gs, jnp.where(gidx > g, 1, 0), 0))
        before = before + ahead
    xm = jnp.where(before < TOPK_GROUPS, x3, neg)

    eidx = lax.broadcasted_iota(I32, (G, P, tm), 0) * P + pidx
    picks = []
    sel = jnp.zeros((G, P, tm), F32)
    for _ in range(TOP_K):
        m = max01(xm)
        ik = min01(jnp.where(xm == m, eidx, N_EXPERTS))
        hit = eidx == ik
        xm = jnp.where(hit, neg, xm)
        sel = jnp.where(hit, 1.0, sel)
        picks.append(ik)

    sel2 = sel.reshape(N_EXPERTS, tm)
    prefix = jnp.dot(sel2.astype(BF16), tri_ref[...], preferred_element_type=F32)
    rank3 = (run_ref[...] + prefix).reshape(G, P, tm)
    run_ref[...] = run_ref[...] + jnp.sum(sel2, axis=1, keepdims=True)
    cnt_ref[...] = run_ref[...].astype(I32)

    krow = lax.broadcasted_iota(I32, (TOP_K, tm), 0)
    idx_o = jnp.zeros((TOP_K, tm), I32)
    rank_o = jnp.zeros((TOP_K, tm), F32)
    gw_o = jnp.zeros((TOP_K, tm), F32)
    for kk, ik in enumerate(picks):
        hit = eidx == ik
        rk = sum01(jnp.where(hit, rank3, 0.0)).reshape(1, tm)
        sk = sum01(jnp.where(hit, sc3, 0.0)).reshape(1, tm)
        idx_o = jnp.where(krow == kk, ik.reshape(1, tm), idx_o)
        rank_o = jnp.where(krow == kk, rk, rank_o)
        gw_o = jnp.where(krow == kk, sk, gw_o)
    gw_o = gw_o / jnp.sum(gw_o, axis=0, keepdims=True) * ROUTED_SCALE
    idx_ref[...] = idx_o
    rank_ref[...] = rank_o.astype(I32)
    gw_ref[...] = gw_o


def router(x, w_router, r_bias):
    n, d = x.shape
    tm = min(ROUTER_TILE, n)
    tri = jnp.triu(jnp.ones((tm, tm), BF16), k=1)
    kspec = pl.BlockSpec((TOP_K, tm), lambda i: (0, i))
    return pl.pallas_call(
        _router_kernel,
        grid=(n // tm,),
        in_specs=[pl.BlockSpec((N_EXPERTS, d), lambda i: (0, 0)),
                  pl.BlockSpec((N_EXPERTS, 1), lambda i: (0, 0)),
                  pl.BlockSpec((tm, tm), lambda i: (0, 0)),
                  pl.BlockSpec((tm, d), lambda i: (i, 0))],
        out_specs=[kspec, kspec, kspec, pl.BlockSpec((N_EXPERTS, 1), lambda i: (0, 0))],
        out_shape=[jax.ShapeDtypeStruct((TOP_K, n), I32), jax.ShapeDtypeStruct((TOP_K, n), I32),
                   jax.ShapeDtypeStruct((TOP_K, n), F32), jax.ShapeDtypeStruct((N_EXPERTS, 1), I32)],
        scratch_shapes=[pltpu.VMEM((N_EXPERTS, 1), F32)],
        compiler_params=_params("arbitrary"),
        name="router",
    )(w_router.T, r_bias.astype(F32).reshape(N_EXPERTS, 1), tri, x)


def _dispatch_kernel(dest_ref, pend_ref, pcnt_ref, xp_ref, xg_ref, zero_ref, sem, *, n_tok):
    i = pl.program_id(0)
    tm = xp_ref.shape[0] * SUBLANE

    def zero_block(start):
        return pltpu.make_async_copy(zero_ref, xg_ref.at[pl.ds(pl.multiple_of(start, MOE_BLOCK), MOE_BLOCK)], sem)

    @pl.when(i == 0)
    def _():
        zero_ref[...] = jnp.zeros_like(zero_ref)
        n_blocks = xg_ref.shape[0] // MOE_BLOCK
        n_used = pend_ref[N_EXPERTS - 1] // MOE_BLOCK

        def start(e, c):
            @pl.when(pcnt_ref[e] > 0)
            def _():
                zero_block(pend_ref[e] - MOE_BLOCK).start()
            return c

        def wait(e, c):
            @pl.when(pcnt_ref[e] > 0)
            def _():
                zero_block(pend_ref[e] - MOE_BLOCK).wait()
            return c

        def start_tail(blk, c):
            zero_block(blk * MOE_BLOCK).start()
            return c

        def wait_tail(blk, c):
            zero_block(blk * MOE_BLOCK).wait()
            return c

        lax.fori_loop(0, N_EXPERTS, start, 0)
        lax.fori_loop(n_used, n_blocks, start_tail, 0)
        lax.fori_loop(0, N_EXPERTS, wait, 0)
        lax.fori_loop(n_used, n_blocks, wait_tail, 0)

    def issue(grp, c):
        for s in range(SUBLANE):
            for k in range(TOP_K):
                d = dest_ref[k * n_tok + i * tm + grp * SUBLANE + s]
                pltpu.make_async_copy(xp_ref.at[grp, pl.ds(s, 1)], xg_ref.at[pl.ds(d, 1)], sem).start()
        return c

    lax.fori_loop(0, tm // SUBLANE, issue, 0)
    for _ in range(TOP_K):
        pltpu.make_async_copy(zero_ref.at[pl.ds(0, tm)], xg_ref.at[pl.ds(0, tm)], sem).wait()


def dispatch(xp, dest, pends, pcounts, *, n_rows):
    n, half = xp.shape
    tm = min(DISPATCH_TILE, n)
    assert tm <= MOE_BLOCK
    grid_spec = pltpu.PrefetchScalarGridSpec(
        num_scalar_prefetch=3,
        grid=(n // tm,),
        in_specs=[pl.BlockSpec((tm // SUBLANE, SUBLANE, half), lambda i, *_: (i, 0, 0))],
        out_specs=pl.BlockSpec(memory_space=pl.ANY),
        scratch_shapes=[pltpu.VMEM((MOE_BLOCK, half), U32), pltpu.SemaphoreType.DMA],
    )
    return pl.pallas_call(
        functools.partial(_dispatch_kernel, n_tok=n),
        grid_spec=grid_spec,
        out_shape=jax.ShapeDtypeStruct((n_rows, half), U32),
        compiler_params=_params("arbitrary"),
        name="moe_dispatch",
    )(dest, pends, pcounts, xp.reshape(n // SUBLANE, SUBLANE, half))


def _expert_kernel(be_ref, nx_ref, nu_ref, x_ref, wg_hbm, wu_hbm, wd_hbm, y_ref,
                   wg_st, wu_st, wd_st, wgu_s, wd_s, sem, *, layer):
    i = pl.program_id(0)
    used = i < nu_ref[0]
    expert = be_ref[i]
    first = jnp.logical_or(i == 0, expert != be_ref[jnp.maximum(i - 1, 0)])

    def fetch(e):
        return (pltpu.make_async_copy(wg_hbm.at[layer, e], wg_st, sem.at[0]),
                pltpu.make_async_copy(wu_hbm.at[layer, e], wu_st, sem.at[1]),
                pltpu.make_async_copy(wd_hbm.at[layer, e], wd_st, sem.at[2]))

    @pl.when(i == 0)
    def _():
        for c in fetch(expert):
            c.start()

    @pl.when(jnp.logical_and(used, first))
    def _():
        for c in fetch(expert):
            c.wait()
        wgu_s[:, :D_EXPERT] = wg_st[...].astype(BF16)
        wgu_s[:, D_EXPERT:] = wu_st[...].astype(BF16)
        wd_s[...] = wd_st[...].astype(BF16)
        nxt = nx_ref[i]

        @pl.when(nxt >= 0)
        def _():
            for c in fetch(nxt):
                c.start()

    @pl.when(used)
    def _():
        sub = MOE_BLOCK // EXPERT_SPLIT
        hs = []
        for r in range(EXPERT_SPLIT):
            lo, hi = _unpack_rows(x_ref[r * sub:(r + 1) * sub, :])
            x = jnp.concatenate([lo, hi], axis=1).astype(BF16)
            hs.append(jnp.dot(x, wgu_s[...], preferred_element_type=F32))
        for r, h in enumerate(hs):
            hid = (_silu(h[:, :D_EXPERT]) * h[:, D_EXPERT:]).astype(BF16)
            y_ref[r * sub:(r + 1) * sub, :] = _pack_rows(jnp.dot(hid, wd_s[...], preferred_element_type=F32))

    @pl.when(jnp.logical_not(used))
    def _():
        y_ref[...] = jnp.zeros_like(y_ref)


def expert_blocks(xg, blk_e, blk_next, n_used, w_gate, w_up, w_down, layer):
    n_rows, half = xg.shape
    d = 2 * half
    n_blocks = n_rows // MOE_BLOCK

    def xmap(i, be, nx, nu):
        return (jnp.minimum(i, nu[0] - 1), 0)

    hbm = pl.BlockSpec(memory_space=pl.ANY)
    grid_spec = pltpu.PrefetchScalarGridSpec(
        num_scalar_prefetch=3,
        grid=(n_blocks,),
        in_specs=[pl.BlockSpec((MOE_BLOCK, half), xmap), hbm, hbm, hbm],
        out_specs=pl.BlockSpec((MOE_BLOCK, half), lambda i, be, nx, nu: (i, 0)),
        scratch_shapes=[pltpu.VMEM((d, D_EXPERT), F32), pltpu.VMEM((d, D_EXPERT), F32),
                        pltpu.VMEM((D_EXPERT, d), F32),
                        pltpu.VMEM((d, 2 * D_EXPERT), BF16), pltpu.VMEM((D_EXPERT, d), BF16),
                        pltpu.SemaphoreType.DMA((3,))],
    )
    return pl.pallas_call(
        functools.partial(_expert_kernel, layer=layer),
        grid_spec=grid_spec,
        out_shape=jax.ShapeDtypeStruct((n_rows, half), U32),
        compiler_params=_params("arbitrary"),
        name="expert_blocks",
    )(blk_e, blk_next, n_used, xg, w_gate, w_up, w_down)


def _dest_kernel(ps_ref, idx_ref, rank_ref, dest_ref):
    idx = idx_ref[...]
    dest = rank_ref[...]
    for e in range(N_EXPERTS):
        dest = dest + jnp.where(idx == e, ps_ref[e], 0)
    dest_ref[...] = dest


def dispatch_rows(idx, rank, pstarts, *, tn=2048):
    k, n = idx.shape
    tn = min(tn, n)
    spec = pl.BlockSpec((k, tn), lambda i, ps: (0, i))
    return pl.pallas_call(
        _dest_kernel,
        grid_spec=pltpu.PrefetchScalarGridSpec(num_scalar_prefetch=1, grid=(n // tn,),
                                               in_specs=[spec, spec], out_specs=spec),
        out_shape=jax.ShapeDtypeStruct((k, n), I32),
        compiler_params=_params("arbitrary"),
        name="dispatch_rows",
    )(pstarts, idx, rank)


def _combine_kernel(dest_ref, x_ref, xb_ref, gw_ref, sg_ref, su_ref, sd_ref, g_ref, b_ref, yg_ref,
                    y_ref, yb_ref, yp_ref, buf_a, buf_b, sem, *, n_tok):
    i = pl.program_id(0)
    n_steps = pl.num_programs(0)
    tm = x_ref.shape[0] // 2
    half = x_ref.shape[1] // 2
    bufs = (buf_a, buf_b)

    def row_copy(tile, which, grp, s, k):
        d = dest_ref[k * n_tok + tile * tm + grp * SUBLANE + s]
        return pltpu.make_async_copy(yg_ref.at[pl.ds(d, 1)], bufs[which].at[k, grp, pl.ds(s, 1)], sem.at[which])

    def wait_tile(which):
        for k in range(TOP_K):
            pltpu.make_async_copy(yg_ref.at[pl.ds(0, tm)], yp_ref.at[pl.ds(0, tm)], sem.at[which]).wait()

    @pl.when(i == 0)
    def _():
        def issue(grp, c):
            for s in range(SUBLANE):
                for k in range(TOP_K):
                    row_copy(0, 0, grp, s, k).start()
            return c
        lax.fori_loop(0, tm // SUBLANE, issue, 0)

    for which in range(2):
        rows = slice(which * tm, (which + 1) * tm)
        wait_tile(which)
        nxt = jnp.minimum(2 * i + which + 1, 2 * n_steps - 1)
        for grp in range(tm // SUBLANE):
            for s in range(SUBLANE):
                for k in range(TOP_K):
                    row_copy(nxt, 1 - which, grp, s, k).start()

        xb = xb_ref[rows, :]
        hg = jnp.dot(xb, sg_ref[...], preferred_element_type=F32)
        hu = jnp.dot(xb, su_ref[...], preferred_element_type=F32)
        hid = (_silu(hg) * hu).astype(BF16)
        f = jnp.dot(hid, sd_ref[...], preferred_element_type=F32)

        gw = gw_ref[rows, :]
        lo = jnp.zeros((tm, half), F32)
        hi = jnp.zeros((tm, half), F32)
        for k in range(TOP_K):
            l_k, h_k = _unpack_rows(bufs[which][k].reshape(tm, half))
            lo = lo + gw[:, k:k + 1] * l_k
            hi = hi + gw[:, k:k + 1] * h_k
        f = f + jnp.concatenate([lo, hi], axis=1)
        y = _layer_norm_rows(DEEPNORM_ALPHA * x_ref[rows, :] + f, g_ref[...], b_ref[...])
        y_ref[rows, :] = y
        yb_ref[rows, :] = y.astype(BF16)
        yp_ref[rows, :] = _pack_rows(y)

    @pl.when(i == n_steps - 1)
    def _():
        wait_tile(0)


def moe_combine(x, xb, gw, dest, yg, s_gate, s_up, s_down, g, b):
    n, d = x.shape
    tm = min(COMBINE_TILE, n // 2)
    once = pl.Buffered(1)
    row = pl.BlockSpec((2 * tm, d), lambda i, *_: (i, 0))
    prow = pl.BlockSpec((2 * tm, d // 2), lambda i, *_: (i, 0))
    vec = pl.BlockSpec((1, d), lambda i, *_: (0, 0), pipeline_mode=once)
    gather_buf = pltpu.VMEM((TOP_K, tm // SUBLANE, SUBLANE, d // 2), U32)
    grid_spec = pltpu.PrefetchScalarGridSpec(
        num_scalar_prefetch=1,
        grid=(n // (2 * tm),),
        in_specs=[row, row, pl.BlockSpec((2 * tm, TOP_K), lambda i, *_: (i, 0)),
                  pl.BlockSpec((d, D_EXPERT), lambda i, *_: (0, 0), pipeline_mode=once),
                  pl.BlockSpec((d, D_EXPERT), lambda i, *_: (0, 0), pipeline_mode=once),
                  pl.BlockSpec((D_EXPERT, d), lambda i, *_: (0, 0), pipeline_mode=once),
                  vec, vec, pl.BlockSpec(memory_space=pl.ANY)],
        out_specs=[row, row, prow],
        scratch_shapes=[gather_buf, gather_buf, pltpu.SemaphoreType.DMA((2,))],
    )
    return pl.pallas_call(
        functools.partial(_combine_kernel, n_tok=n),
        grid_spec=grid_spec,
        out_shape=[jax.ShapeDtypeStruct((n, d), F32), jax.ShapeDtypeStruct((n, d), BF16),
                   jax.ShapeDtypeStruct((n, d // 2), U32)],
        compiler_params=_params("arbitrary"),
        name="moe_combine",
    )(dest, x, xb, gw, s_gate, s_up, s_down, g.reshape(1, d), b.reshape(1, d), yg)


def moe_ffn(x, xb, xp, layer, w_router, r_bias, w_gate, w_up, w_down, s_gate, s_up, s_down, ln_g, ln_b):
    n_tok = x.shape[0]
    idx, rank, gw, counts = router(x, w_router, r_bias)
    counts = counts.reshape(N_EXPERTS)
    pcounts = (counts + MOE_BLOCK - 1) // MOE_BLOCK * MOE_BLOCK
    pends = jnp.cumsum(pcounts).astype(I32)
    pstarts = pends - pcounts
    n_blocks = n_tok * TOP_K // MOE_BLOCK + N_EXPERTS
    blk_start = jnp.arange(n_blocks, dtype=I32) * MOE_BLOCK
    blk_e = jnp.minimum(jnp.sum(pends[None, :] <= blk_start[:, None], axis=1), N_EXPERTS - 1).astype(I32)
    n_used = (pends[-1:] // MOE_BLOCK).astype(I32)
    eids = jnp.arange(N_EXPERTS, dtype=I32)
    later = jnp.where((eids[None, :] > eids[:, None]) & (pcounts[None, :] > 0), eids[None, :], N_EXPERTS)
    nxt_e = jnp.min(later, axis=1)
    nxt_e = jnp.where(nxt_e == N_EXPERTS, -1, nxt_e)
    blk_next = jnp.sum(jnp.where(blk_e[:, None] == eids[None, :], nxt_e[None, :], 0), axis=1).astype(I32)
    dest = dispatch_rows(idx, rank, pstarts.astype(I32)).reshape(TOP_K * n_tok)

    xg = dispatch(xp, dest, pends, pcounts.astype(I32), n_rows=n_blocks * MOE_BLOCK)
    yg = expert_blocks(xg, blk_e, blk_next, n_used, w_gate, w_up, w_down, layer)
    return moe_combine(x, xb, gw.T, dest, yg, s_gate, s_up, s_down, ln_g, ln_b)


def kernel(x, ln_g, ln_b, ev_w_in, ev_decay_exp, ev_gn_w, ev_rpb, ev_w_out, od_w_in, od_lb, od_norm_w, od_w_out, moe_w_router, moe_bias, moe_w_gate, moe_w_up, moe_w_down, sh_w_gate, sh_w_up, sh_w_down):
    batch, seq, d = x.shape
    n = batch * seq
    xf = x.reshape(n, d).astype(F32)
    xb = xf.astype(BF16)
    for layer in range(DEPTH):
        j = layer // 2
        if layer % 2 == 0:
            ph = matmul(xb, ev_w_in, j, out_dtype=BF16, heads=True)
            o_ret = retention(ph, ev_decay_exp[j], ev_gn_w[j].astype(F32), batch=batch, seq=seq)
            o_na = neighborhood_attention(ph, ev_rpb[j], batch=batch, seq=seq, slab0=4 * RET_W // LANE)
            mixed = jnp.concatenate([o_ret, o_na], axis=1)
            w_out = ev_w_out
        else:
            ph = matmul(xb, od_w_in, j, out_dtype=BF16, heads=True)
            mixed = hgrn2(ph, od_lb, od_norm_w[j].astype(F32), layer=layer, batch=batch, seq=seq)
            w_out = od_w_out
        m = matmul(mixed, w_out, j, out_dtype=F32, heads=False)
        xf, xb, xp = ln_residual(xf, m, ln_g[layer, 0].astype(F32), ln_b[layer, 0].astype(F32))
        xf, xb, xp = moe_ffn(xf, xb, xp, layer, moe_w_router[layer].astype(F32), moe_bias[layer],
                             moe_w_gate, moe_w_up, moe_w_down, sh_w_gate[layer].astype(BF16),
                             sh_w_up[layer].astype(BF16), sh_w_down[layer].astype(BF16),
                             ln_g[layer, 1].astype(F32), ln_b[layer, 1].astype(F32))
    return xf.reshape(batch, seq, d)
```

```python
import functools

import numpy as np
import jax
import jax.numpy as jnp
from jax import lax
from jax.experimental import pallas as pl
from jax.experimental.pallas import tpu as pltpu

F32 = jnp.float32
BF16 = jnp.bfloat16
U32 = jnp.uint32
I32 = jnp.int32

LANE = 128
SUBLANE = 8
VMEM_LIMIT = 56 * 1024 * 1024

DEPTH = 2
GRID_W = 64
RET_HEADS = 8
RET_DK = 256
RET_W = RET_HEADS * RET_DK
RET_CHUNK = 128
ROPE_BASE = 10000.0
NA_HEADS = 16
NA_DH = 128
NA_W = NA_HEADS * NA_DH
NA_KH = 8
NA_KW = 16
NA_UNROLL = 16
HG_HEADS = 32
HG_DK = 128
HG_W = HG_HEADS * HG_DK
HG_CHUNK = 128
HG_SUB = SUBLANE
HG_ROWS = 2048
HG_UNROLL = 8
N_EXPERTS = 64
TOP_K = 8
N_GROUPS = 8
TOPK_GROUPS = 4
E_PER_GROUP = N_EXPERTS // N_GROUPS
D_EXPERT = 384
ROUTED_SCALE = 2.5
MOE_BLOCK = 512
EXPERT_SPLIT = 2
ROUTER_TILE = 512
DISPATCH_TILE = 256
COMBINE_TILE = 128
ISSUE_UNROLL = 8
DEEPNORM_ALPHA = (2.0 * DEPTH) ** 0.25
LN_EPS = 1e-5
NORM_EPS = 1e-6
MASK_NEG = -1e30
LOG2_E = 1.4426950408889634


def _params(*sem):
    return pltpu.CompilerParams(dimension_semantics=sem, vmem_limit_bytes=VMEM_LIMIT)


def _sigmoid_pair(z):
    e = jnp.exp(-jnp.abs(z))
    r = 1.0 / (1.0 + e)
    er = e * r
    pos = z >= 0
    return jnp.where(pos, r, er), jnp.where(pos, er, r)


def _silu(z):
    return z * (0.5 * jnp.tanh(0.5 * z) + 0.5)


def _pack_rows(y):
    half = y.shape[1] // 2
    lo = lax.bitcast_convert_type(y[:, :half].astype(BF16).astype(F32), U32) >> 16
    hi = lax.bitcast_convert_type(y[:, half:].astype(BF16).astype(F32), U32) & jnp.uint32(0xFFFF0000)
    return hi | lo


def _unpack_rows(u):
    lo = lax.bitcast_convert_type(u << 16, F32)
    hi = lax.bitcast_convert_type(u & jnp.uint32(0xFFFF0000), F32)
    return lo, hi


def _mm_kernel(a_ref, w_ref, o_ref, wb_ref, *, heads):
    @pl.when(pl.program_id(1) == 0)
    def _():
        wb_ref[...] = w_ref[...].astype(BF16)

    acc = jnp.dot(a_ref[...], wb_ref[...], preferred_element_type=F32)
    if heads:
        for j in range(o_ref.shape[0]):
            o_ref[j] = acc[:, j * LANE:(j + 1) * LANE].astype(o_ref.dtype)
    else:
        o_ref[...] = acc.astype(o_ref.dtype)


def matmul(a, w, layer, *, out_dtype, heads, tm=1024, tn=512):
    n, k = a.shape
    m = w.shape[2]
    tm, tn = min(tm, n), min(tn, m)
    if heads:
        out_shape = jax.ShapeDtypeStruct((m // LANE, n, LANE), out_dtype)
        out_spec = pl.BlockSpec((tn // LANE, tm, LANE), lambda j, i: (j, i, 0))
    else:
        out_shape = jax.ShapeDtypeStruct((n, m), out_dtype)
        out_spec = pl.BlockSpec((tm, tn), lambda j, i: (i, j))
    return pl.pallas_call(
        functools.partial(_mm_kernel, heads=heads),
        grid=(m // tn, n // tm),
        in_specs=[pl.BlockSpec((tm, k), lambda j, i: (i, 0)),
                  pl.BlockSpec((None, k, tn), lambda j, i: (layer, 0, j))],
        out_specs=out_spec,
        out_shape=out_shape,
        scratch_shapes=[pltpu.VMEM((k, tn), BF16)],
        compiler_params=_params("arbitrary", "arbitrary"),
        name="proj_matmul",
    )(a, w)


def _layer_norm_rows(z, g, b):
    mu = jnp.mean(z, axis=-1, keepdims=True)
    zc = z - mu
    var = jnp.mean(zc * zc, axis=-1, keepdims=True)
    return zc * lax.rsqrt(var + LN_EPS) * g + b


def _ln_kernel(x_ref, m_ref, g_ref, b_ref, y_ref, yb_ref, yp_ref):
    z = DEEPNORM_ALPHA * x_ref[...] + m_ref[...].astype(F32)
    y = _layer_norm_rows(z, g_ref[...], b_ref[...])
    y_ref[...] = y
    yb_ref[...] = y.astype(BF16)
    yp_ref[...] = _pack_rows(y)


def ln_residual(x, m, g, b, *, tm=256):
    n, d = x.shape
    tm = min(tm, n)
    row = pl.BlockSpec((tm, d), lambda i: (i, 0))
    prow = pl.BlockSpec((tm, d // 2), lambda i: (i, 0))
    vec = pl.BlockSpec((1, d), lambda i: (0, 0))
    return pl.pallas_call(
        _ln_kernel,
        grid=(n // tm,),
        in_specs=[row, row, vec, vec],
        out_specs=[row, row, prow],
        out_shape=[jax.ShapeDtypeStruct((n, d), F32), jax.ShapeDtypeStruct((n, d), BF16),
                   jax.ShapeDtypeStruct((n, d // 2), U32)],
        compiler_params=_params("arbitrary"),
        name="ln_residual",
    )(x, m, g.reshape(1, d), b.reshape(1, d))


def _ret_kernel(lg_ref, q_ref, k_ref, v_ref, cos_ref, sin_ref, *rest, rev, n_chunks):
    if rev:
        oa_ref, g_ref, gn_ref, out_ref, s_ref = rest
    else:
        out_ref, s_ref = rest
    C = RET_CHUNK
    h = pl.program_id(1)

    @pl.when(pl.program_id(2) == 0)
    def _():
        s_ref[...] = jnp.zeros_like(s_ref)

    lg = lg_ref[1 if rev else 0, h]
    row = lax.broadcasted_iota(I32, (C, RET_DK), 0).astype(F32)
    if rev:
        q_dec = jnp.exp(lg * (C - row))
        k_dec = jnp.exp(lg * row)
    else:
        q_dec = jnp.exp(lg * (row + 1.0))
        k_dec = jnp.exp(lg * (C - 1.0 - row))
        dist = (lax.broadcasted_iota(I32, (C, C), 0)
                - lax.broadcasted_iota(I32, (C, C), 1)).astype(F32)
        decay = jnp.where(dist >= 0,
                          jnp.exp(lg_ref[0, h] * jnp.maximum(dist, 0.0)),
                          jnp.exp(lg_ref[1, h] * jnp.maximum(-dist, 0.0)))
    chunk_decay = jnp.exp(jnp.full((1, 1), C, F32) * lg)

    def rot(ref, sl, cos, sin):
        t1 = ref[0, sl, :].astype(F32)
        t2 = ref[1, sl, :].astype(F32)
        return jnp.concatenate([t1 * cos - t2 * sin, t1 * sin + t2 * cos], axis=1)

    def local_part(sl):
        cos, sin = cos_ref[sl, :], sin_ref[sl, :]
        qr = rot(q_ref, sl, cos, sin)
        kr = rot(k_ref, sl, cos, sin) * (RET_DK ** -0.5)
        v = jnp.concatenate([v_ref[0, sl, :], v_ref[1, sl, :]], axis=1)
        kv = lax.dot_general((kr * k_dec).astype(BF16), v, (((0,), (0,)), ((), ())),
                             preferred_element_type=F32)
        if rev:
            o = jnp.concatenate([oa_ref[0, sl, :], oa_ref[1, sl, :]], axis=1)
        else:
            scores = lax.dot_general(qr.astype(BF16), kr.astype(BF16), (((1,), (1,)), ((), ())),
                                     preferred_element_type=F32) * decay
            o = jnp.dot(scores.astype(BF16), v, preferred_element_type=F32)
        return o, (qr * q_dec).astype(BF16), kv

    def carried_part(sl, local):
        o, q_in, kv = local
        state = s_ref[...]
        o = o + jnp.dot(q_in, state.astype(BF16), preferred_element_type=F32)
        s_ref[...] = chunk_decay * state + kv
        if rev:
            mu = jnp.mean(o, axis=-1, keepdims=True)
            oc = o - mu
            var = jnp.mean(oc * oc, axis=-1, keepdims=True)
            gate = jnp.concatenate([g_ref[0, sl, :], g_ref[1, sl, :]], axis=1).astype(F32)
            res = oc * lax.rsqrt(var + NORM_EPS) * gn_ref[...] * _silu(gate)
            out_ref[sl, :] = res.astype(out_ref.dtype)
        else:
            out_ref[0, sl, :] = o[:, :LANE]
            out_ref[1, sl, :] = o[:, LANE:]

    order = range(n_chunks - 1, -1, -1) if rev else range(n_chunks)
    slices = [slice(ci * C, (ci + 1) * C) for ci in order]
    locals_ = [local_part(sl) for sl in slices]
    for sl, local in zip(slices, locals_):
        carried_part(sl, local)


def _ret_pass(lg, ph, cos, sin, extra, *, rev, batch, seq, rows):
    n_blk = seq // rows
    n_chunks = rows // RET_CHUNK
    H = RET_HEADS

    def blk(step):
        return (n_blk - 1 - step) if rev else step

    def slab(group):
        return pl.BlockSpec((2, rows, LANE), lambda b, h, s: (group * H + h, b * n_blk + blk(s), 0))

    tab = pl.BlockSpec((rows, LANE), lambda b, h, s: (blk(s), 0))
    in_specs = [pl.BlockSpec(memory_space=pltpu.SMEM), slab(0), slab(1), slab(2), tab, tab]
    args = [lg, ph, ph, ph, cos, sin]
    if rev:
        oa, gn_w = extra
        in_specs += [pl.BlockSpec((2, rows, LANE), lambda b, h, s: (h, b * n_blk + blk(s), 0)),
                     slab(3),
                     pl.BlockSpec((1, RET_DK), lambda b, h, s: (0, h))]
        args += [oa, ph, gn_w.reshape(1, RET_W)]
        out_shape = jax.ShapeDtypeStruct((batch * seq, RET_W), BF16)
        out_spec = pl.BlockSpec((rows, RET_DK), lambda b, h, s: (b * n_blk + blk(s), h))
    else:
        out_shape = jax.ShapeDtypeStruct((2 * H, batch * seq, LANE), F32)
        out_spec = pl.BlockSpec((2, rows, LANE), lambda b, h, s: (h, b * n_blk + blk(s), 0))
    return pl.pallas_call(
        functools.partial(_ret_kernel, rev=rev, n_chunks=n_chunks),
        grid=(batch, H, n_blk),
        in_specs=in_specs,
        out_specs=out_spec,
        out_shape=out_shape,
        scratch_shapes=[pltpu.VMEM((RET_DK, RET_DK), F32)],
        compiler_params=_params("arbitrary", "arbitrary", "arbitrary"),
        name="retention_bwd" if rev else "retention_fwd",
    )(*args)


def retention(ph, decay_exp, gn_w, *, batch, seq, rows=1024):
    rows = min(rows, seq)
    log_g = jnp.log1p(-jnp.exp2(-decay_exp.astype(F32)))
    inv = ROPE_BASE ** (-jnp.linspace(0.0, 1.0, RET_DK // 2, dtype=F32))
    ang = jnp.arange(seq, dtype=F32)[:, None] * inv[None, :]
    cos, sin = jnp.cos(ang), jnp.sin(ang)
    o_a = _ret_pass(log_g, ph, cos, sin, None, rev=False, batch=batch, seq=seq, rows=rows)
    return _ret_pass(log_g, ph, cos, sin, (o_a, gn_w), rev=True, batch=batch, seq=seq, rows=rows)


def _na_kernel(q_ref, k_ref, v_ref, tab_ref, o_ref, *, grid_rows):
    span = NA_KH * GRID_W
    scale = NA_DH ** -0.5

    def body(it, carry):
        rows = [it * NA_UNROLL + j for j in range(NA_UNROLL)]
        starts = [jnp.clip(r - NA_KH // 2, 0, grid_rows - NA_KH) for r in rows]
        scores = []
        for r, r0 in zip(rows, starts):
            q = q_ref[0, pl.ds(pl.multiple_of(r * GRID_W, GRID_W), GRID_W), :]
            kw = k_ref[0, pl.ds(pl.multiple_of(r0 * GRID_W, GRID_W), span), :]
            s = lax.dot_general(q, kw, (((1,), (1,)), ((), ())), preferred_element_type=F32)
            scores.append(s * scale + tab_ref[0, r0 - r + NA_KH - 1])
        probs = []
        for s in scores:
            p = jnp.exp(s - jnp.max(s, axis=-1, keepdims=True))
            probs.append((p.astype(BF16), jnp.sum(p, axis=-1, keepdims=True)))
        for r, r0, (p, l) in zip(rows, starts, probs):
            vw = v_ref[0, pl.ds(pl.multiple_of(r0 * GRID_W, GRID_W), span), :]
            o = jnp.dot(p, vw, preferred_element_type=F32) / l
            o_ref[pl.ds(pl.multiple_of(r * GRID_W, GRID_W), GRID_W), :] = o.astype(o_ref.dtype)
        return carry

    lax.fori_loop(0, grid_rows // NA_UNROLL, body, 0)


def _na_bias_table(rpb):
    qc = np.arange(GRID_W)[:, None]
    kc = np.arange(GRID_W)[None, :]
    wstart = np.clip(qc - NA_KW // 2, 0, GRID_W - NA_KW)
    valid = (kc >= wstart) & (kc < wstart + NA_KW)
    dc = np.clip(kc - qc, -(NA_KW - 1), NA_KW - 1) + NA_KW - 1
    onehot = ((dc[None] == np.arange(2 * NA_KW - 1)[:, None, None]) & valid[None]).astype(np.float32)
    by_col = jnp.einsum('hrj,jck->hrck', rpb, jnp.asarray(onehot), precision=lax.Precision.HIGHEST)
    by_col = by_col + jnp.asarray(np.where(valid, 0.0, MASK_NEG).astype(np.float32))[None, None]
    tab = jnp.stack([by_col[:, off:off + NA_KH] for off in range(NA_KH)], axis=1)
    return tab.transpose(0, 1, 3, 2, 4).reshape(NA_HEADS, NA_KH, GRID_W, NA_KH * GRID_W)


def neighborhood_attention(ph, rpb, *, batch, seq, slab0):
    grid_rows = seq // GRID_W
    assert grid_rows >= NA_KH and grid_rows % NA_UNROLL == 0
    tab = _na_bias_table(rpb.astype(F32))

    def slab(group):
        return pl.BlockSpec((1, seq, LANE), lambda b, h: (slab0 + group * NA_HEADS + h, b, 0))

    return pl.pallas_call(
        functools.partial(_na_kernel, grid_rows=grid_rows),
        grid=(batch, NA_HEADS),
        in_specs=[slab(0), slab(1), slab(2),
                  pl.BlockSpec((1, NA_KH, GRID_W, NA_KH * GRID_W), lambda b, h: (h, 0, 0, 0))],
        out_specs=pl.BlockSpec((seq, NA_DH), lambda b, h: (b, h)),
        out_shape=jax.ShapeDtypeStruct((batch * seq, NA_W), BF16),
        compiler_params=_params("arbitrary", "arbitrary"),
        name="neighborhood_attention",
    )(ph, ph, ph, tab)


def _hg_widths():
    widths, w = [], HG_CHUNK // 2
    while w >= HG_SUB:
        widths.append(w)
        w //= 2
    return widths


def _hg_kernel(lb_ref, zq_ref, zf_ref, zi_ref, *rest, rev, layer, n_chunks):
    if rev:
        of_ref, zg_ref, nw_ref, out_ref, st_ref, lvl_ref, dg_ref, bq_ref = rest
    else:
        out_ref, st_ref, lvl_ref, dg_ref, bq_ref = rest
    C = HG_CHUNK
    U = HG_SUB
    widths = _hg_widths()

    @pl.when(pl.program_id(2) == 0)
    def _():
        st_ref[...] = jnp.zeros_like(st_ref)
        s_i = lax.broadcasted_iota(I32, (C, C), 0)
        t_i = lax.broadcasted_iota(I32, (C, C), 1)
        earlier = (s_i > t_i) if rev else (s_i < t_i)
        split = jnp.where(earlier, s_i ^ t_i, 0)
        for l, w in enumerate(widths):
            lvl_ref[l] = jnp.where(split >= w, jnp.where(split < 2 * w, 1.0, 0.0), 0.0)
        near = jnp.where((s_i >= t_i) if rev else (s_i <= t_i), jnp.where((s_i ^ t_i) < U, 1.0, 0.0), 0.0)
        for tt in range(U):
            dg_ref[tt] = near * jnp.where((t_i & (U - 1)) == tt, 1.0, 0.0)

    lbp = lb_ref[...]
    pe = jnp.exp(lbp - jnp.max(lbp, axis=0, keepdims=True))
    share = pe / jnp.sum(pe, axis=0, keepdims=True)
    lb = jnp.sum(share[:layer + 1], axis=0, keepdims=True) - share[0:1]
    one_m_lb = 1.0 - lb

    ri = lax.broadcasted_iota(I32, (C, C), 0)
    ci_ = lax.broadcasted_iota(I32, (C, C), 1)
    tri = jnp.where((ri <= ci_) if rev else (ri >= ci_), 1.0, 0.0).astype(BF16)
    row = lax.broadcasted_iota(I32, (C, HG_DK), 0)

    def rows_of(step):
        cidx = (n_chunks - 1 - step) if rev else step
        return pl.ds(pl.multiple_of(cidx * C, C), C)

    def local_part(sl, slot):
        q = _silu(zq_ref[0, sl, :].astype(F32))
        sig, nsig = _sigmoid_pair(zf_ref[0, sl, :].astype(F32))
        k = one_m_lb * nsig
        lf = jnp.log(lb + one_m_lb * sig) * LOG2_E
        v = zi_ref[0, sl, :]

        hi = lf.astype(BF16)
        r1 = lf - hi.astype(F32)
        mid = r1.astype(BF16)
        lo = (r1 - mid.astype(F32)).astype(BF16)
        parts = jnp.dot(tri, jnp.concatenate([hi, mid, lo], axis=1), preferred_element_type=F32)
        b = parts[:, :LANE] + parts[:, LANE:2 * LANE] + parts[:, 2 * LANE:]

        att_t = jnp.zeros((C, C), F32)
        for l, w in enumerate(widths):
            b3 = b.reshape(C // (2 * w), 2 * w, HG_DK)
            edge = b3[:, w:w + 1, :] if rev else b3[:, w - 1:w, :]
            d = (b3 - edge).reshape(C, HG_DK)
            is_q = ((row & w) == 0) if rev else ((row & w) != 0)
            x = (jnp.where(is_q, q, k) * jnp.exp2(jnp.where(is_q, d, -d))).astype(BF16)
            a = lax.dot_general(x, x, (((1,), (1,)), ((), ())), preferred_element_type=F32)
            att_t = att_t + a * lvl_ref[l]

        bq_ref[slot, 0] = b
        bq_ref[slot, 1] = q

        def step_rows(which, tt):
            return jnp.concatenate(
                [jnp.broadcast_to(bq_ref[slot, which, pl.ds(n * U + tt, 1), :], (U, HG_DK))
                 for n in range(C // U)], axis=0)

        for tt in range(U):
            e = jnp.exp2(jnp.minimum(step_rows(0, tt) - b, 0.0))
            a = jnp.sum(e * k * step_rows(1, tt), axis=-1, keepdims=True)
            att_t = att_t + a * dg_ref[tt]
        o = lax.dot_general(att_t.astype(BF16), v, (((0,), (0,)), ((), ())), preferred_element_type=F32)

        b_end = b[0:1, :] if rev else b[C - 1:C, :]
        kd = (k * jnp.exp2(b_end - b)).astype(BF16)
        kv = lax.dot_general(v, kd, (((0,), (0,)), ((), ())), preferred_element_type=F32)
        return o, (q * jnp.exp2(b)).astype(BF16), kv, jnp.exp2(b_end)

    def carried_part(sl, local):
        o, q_dec, kv, end_decay = local
        state_t = st_ref[...]
        o = o + lax.dot_general(q_dec, state_t.astype(BF16), (((1,), (1,)), ((), ())),
                                preferred_element_type=F32)
        st_ref[...] = state_t * end_decay + kv
        if rev:
            o = o + of_ref[0, sl, :]
            o = o * lax.rsqrt(jnp.mean(o * o, axis=-1, keepdims=True) + NORM_EPS)
            res = o * nw_ref[...] * _silu(zg_ref[0, sl, :].astype(F32))
            out_ref[sl, :] = res.astype(out_ref.dtype)
        else:
            out_ref[0, sl, :] = o

    def group(it, carry):
        slices = [rows_of(it * HG_UNROLL + j) for j in range(HG_UNROLL)]
        locals_ = [local_part(sl, slot) for slot, sl in enumerate(slices)]
        for sl, local in zip(slices, locals_):
            carried_part(sl, local)
        return carry

    lax.fori_loop(0, n_chunks // HG_UNROLL, group, 0)


def _hg_pass(od_lb, ph, extra, *, rev, layer, batch, seq, rows):
    n_blk = seq // rows
    H = HG_HEADS

    def blk(step):
        return (n_blk - 1 - step) if rev else step

    def slab(group):
        return pl.BlockSpec((1, rows, LANE), lambda b, h, s: (group * H + h, b * n_blk + blk(s), 0))

    in_specs = [pl.BlockSpec((od_lb.shape[0], HG_DK), lambda b, h, s: (0, h)),
                slab(0), slab(2 if rev else 1), slab(3)]
    args = [od_lb, ph, ph, ph]
    if rev:
        o_f, norm_w = extra
        in_specs += [pl.BlockSpec((1, rows, LANE), lambda b, h, s: (h, b * n_blk + blk(s), 0)),
                     slab(4),
                     pl.BlockSpec((1, HG_DK), lambda b, h, s: (0, h))]
        args += [o_f, ph, norm_w.reshape(1, HG_W)]
        out_shape = jax.ShapeDtypeStruct((batch * seq, HG_W), BF16)
        out_spec = pl.BlockSpec((rows, HG_DK), lambda b, h, s: (b * n_blk + blk(s), h))
    else:
        out_shape = jax.ShapeDtypeStruct((H, batch * seq, LANE), F32)
        out_spec = pl.BlockSpec((1, rows, LANE), lambda b, h, s: (h, b * n_blk + blk(s), 0))
    return pl.pallas_call(
        functools.partial(_hg_kernel, rev=rev, layer=layer, n_chunks=rows // HG_CHUNK),
        grid=(batch, H, n_blk),
        in_specs=in_specs,
        out_specs=out_spec,
        out_shape=out_shape,
        scratch_shapes=[pltpu.VMEM((HG_DK, HG_DK), F32),
                        pltpu.VMEM((len(_hg_widths()), HG_CHUNK, HG_CHUNK), F32),
                        pltpu.VMEM((HG_SUB, HG_CHUNK, HG_CHUNK), F32),
                        pltpu.VMEM((HG_UNROLL, 2, HG_CHUNK, HG_DK), F32)],
        compiler_params=_params("arbitrary", "arbitrary", "arbitrary"),
        name="hgrn2_bwd" if rev else "hgrn2_fwd",
    )(*args)


def hgrn2(ph, od_lb, norm_w, *, layer, batch, seq):
    rows = min(HG_ROWS, seq)
    od_lb = od_lb.astype(F32)
    o_f = _hg_pass(od_lb, ph, None, rev=False, layer=layer, batch=batch, seq=seq, rows=rows)
    return _hg_pass(od_lb, ph, (o_f, norm_w), rev=True, layer=layer, batch=batch, seq=seq, rows=rows)


def _router_kernel(wt_ref, bias_ref, tri_ref, x_ref, idx_ref, rank_ref, gw_ref, cnt_ref, run_ref):
    tm = x_ref.shape[0]
    G, P = N_GROUPS, E_PER_GROUP

    @pl.when(pl.program_id(0) == 0)
    def _():
        run_ref[...] = jnp.zeros_like(run_ref)

    def split(a):
        hi = a.astype(BF16)
        return hi, (a - hi.astype(F32)).astype(BF16)

    def nt_dot(a, b):
        return lax.dot_general(a, b, (((1,), (1,)), ((), ())), preferred_element_type=F32)

    w_hi, w_lo = split(wt_ref[...])
    x_hi, x_lo = split(x_ref[...])
    logits = nt_dot(w_hi, x_hi) + (nt_dot(w_lo, x_hi) + nt_dot(w_hi, x_lo))
    scores = _sigmoid_pair(logits)[0]
    sc3 = scores.reshape(G, P, tm)
    x3 = (scores + bias_ref[...]).reshape(G, P, tm)
    neg = -jnp.inf

    def max01(a):
        return jnp.max(jnp.max(a, axis=0, keepdims=True), axis=1, keepdims=True)

    def min01(a):
        return jnp.min(jnp.min(a, axis=0, keepdims=True), axis=1, keepdims=True)

    def sum01(a):
        return jnp.sum(jnp.sum(a, axis=0, keepdims=True), axis=1, keepdims=True)

    pidx = lax.broadcasted_iota(I32, (G, P, tm), 1)
    m1 = jnp.max(x3, axis=1, keepdims=True)
    i1 = jnp.min(jnp.where(x3 == m1, pidx, P), axis=1, keepdims=True)
    m2 = jnp.max(jnp.where(pidx == i1, neg, x3), axis=1, keepdims=True)
    gs = m1 + m2
    gidx = lax.broadcasted_iota(I32, (G, 1, tm), 0)
    before = jnp.zeros((G, 1, tm), I32)
    for g in range(G):
        other = gs[g:g + 1]
        ahead = jnp.where(other > gs, 1, jnp.where(other == gs, jnp.where(gidx > g, 1, 0), 0))
        before = before + ahead
    xm = jnp.where(before < TOPK_GROUPS, x3, neg)

    eidx = lax.broadcasted_iota(I32, (G, P, tm), 0) * P + pidx
    picks = []
    sel = jnp.zeros((G, P, tm), F32)
    for _ in range(TOP_K):
        m = max01(xm)
        ik = min01(jnp.where(xm == m, eidx, N_EXPERTS))
        hit = eidx == ik
        xm = jnp.where(hit, neg, xm)
        sel = jnp.where(hit, 1.0, sel)
        picks.append(ik)

    sel2 = sel.reshape(N_EXPERTS, tm)
    prefix = jnp.dot(sel2.astype(BF16), tri_ref[...], preferred_element_type=F32)
    rank3 = (run_ref[...] + prefix).reshape(G, P, tm)
    run_ref[...] = run_ref[...] + jnp.sum(sel2, axis=1, keepdims=True)
    cnt_ref[...] = run_ref[...].astype(I32)

    krow = lax.broadcasted_iota(I32, (TOP_K, tm), 0)
    idx_o = jnp.zeros((TOP_K, tm), I32)
    rank_o = jnp.zeros((TOP_K, tm), F32)
    gw_o = jnp.zeros((TOP_K, tm), F32)
    for kk, ik in enumerate(picks):
        hit = eidx == ik
        rk = sum01(jnp.where(hit, rank3, 0.0)).reshape(1, tm)
        sk = sum01(jnp.where(hit, sc3, 0.0)).reshape(1, tm)
        idx_o = jnp.where(krow == kk, ik.reshape(1, tm), idx_o)
        rank_o = jnp.where(krow == kk, rk, rank_o)
        gw_o = jnp.where(krow == kk, sk, gw_o)
    gw_o = gw_o / jnp.sum(gw_o, axis=0, keepdims=True) * ROUTED_SCALE
    idx_ref[...] = idx_o
    rank_ref[...] = rank_o.astype(I32)
    gw_ref[...] = gw_o


def router(x, w_router, r_bias):
    n, d = x.shape
    tm = min(ROUTER_TILE, n)
    tri = jnp.triu(jnp.ones((tm, tm), BF16), k=1)
    kspec = pl.BlockSpec((TOP_K, tm), lambda i: (0, i))
    return pl.pallas_call(
        _router_kernel,
        grid=(n // tm,),
        in_specs=[pl.BlockSpec((N_EXPERTS, d), lambda i: (0, 0)),
                  pl.BlockSpec((N_EXPERTS, 1), lambda i: (0, 0)),
                  pl.BlockSpec((tm, tm), lambda i: (0, 0)),
                  pl.BlockSpec((tm, d), lambda i: (i, 0))],
        out_specs=[kspec, kspec, kspec, pl.BlockSpec((N_EXPERTS, 1), lambda i: (0, 0))],
        out_shape=[jax.ShapeDtypeStruct((TOP_K, n), I32), jax.ShapeDtypeStruct((TOP_K, n), I32),
                   jax.ShapeDtypeStruct((TOP_K, n), F32), jax.ShapeDtypeStruct((N_EXPERTS, 1), I32)],
        scratch_shapes=[pltpu.VMEM((N_EXPERTS, 1), F32)],
        compiler_params=_params("arbitrary"),
        name="router",
    )(w_router.T, r_bias.astype(F32).reshape(N_EXPERTS, 1), tri, x)


def _dispatch_kernel(dest_ref, pend_ref, pcnt_ref, xp_ref, xg_ref, zero_ref, sem, *, n_tok):
    i = pl.program_id(0)
    tm = xp_ref.shape[0] * SUBLANE

    def zero_block(start):
        return pltpu.make_async_copy(zero_ref, xg_ref.at[pl.ds(pl.multiple_of(start, MOE_BLOCK), MOE_BLOCK)], sem)

    @pl.when(i == 0)
    def _():
        zero_ref[...] = jnp.zeros_like(zero_ref)
        n_blocks = xg_ref.shape[0] // MOE_BLOCK
        n_used = pend_ref[N_EXPERTS - 1] // MOE_BLOCK

        def start(e, c):
            @pl.when(pcnt_ref[e] > 0)
            def _():
                zero_block(pend_ref[e] - MOE_BLOCK).start()
            return c

        def wait(e, c):
            @pl.when(pcnt_ref[e] > 0)
            def _():
                zero_block(pend_ref[e] - MOE_BLOCK).wait()
            return c

        def start_tail(blk, c):
            zero_block(blk * MOE_BLOCK).start()
            return c

        def wait_tail(blk, c):
            zero_block(blk * MOE_BLOCK).wait()
            return c

        lax.fori_loop(0, N_EXPERTS, start, 0)
        lax.fori_loop(n_used, n_blocks, start_tail, 0)
        lax.fori_loop(0, N_EXPERTS, wait, 0)
        lax.fori_loop(n_used, n_blocks, wait_tail, 0)

    def issue(grp, c):
        for s in range(SUBLANE):
            for k in range(TOP_K):
                d = dest_ref[k * n_tok + i * tm + grp * SUBLANE + s]
                pltpu.make_async_copy(xp_ref.at[grp, pl.ds(s, 1)], xg_ref.at[pl.ds(d, 1)],
                                      sem).start(priority=k % 2)
        return c

    lax.fori_loop(0, tm // SUBLANE, issue, 0)
    for _ in range(TOP_K):
        pltpu.make_async_copy(zero_ref.at[pl.ds(0, tm)], xg_ref.at[pl.ds(0, tm)], sem).wait()


def dispatch(xp, dest, pends, pcounts, *, n_rows):
    n, half = xp.shape
    tm = min(DISPATCH_TILE, n)
    assert tm <= MOE_BLOCK
    grid_spec = pltpu.PrefetchScalarGridSpec(
        num_scalar_prefetch=3,
        grid=(n // tm,),
        in_specs=[pl.BlockSpec((tm // SUBLANE, SUBLANE, half), lambda i, *_: (i, 0, 0))],
        out_specs=pl.BlockSpec(memory_space=pl.ANY),
        scratch_shapes=[pltpu.VMEM((MOE_BLOCK, half), U32), pltpu.SemaphoreType.DMA],
    )
    return pl.pallas_call(
        functools.partial(_dispatch_kernel, n_tok=n),
        grid_spec=grid_spec,
        out_shape=jax.ShapeDtypeStruct((n_rows, half), U32),
        compiler_params=_params("arbitrary"),
        name="moe_dispatch",
    )(dest, pends, pcounts, xp.reshape(n // SUBLANE, SUBLANE, half))


def _expert_kernel(be_ref, nx_ref, nu_ref, x_ref, wg_hbm, wu_hbm, wd_hbm, y_ref,
                   wg_st, wu_st, wd_st, wgu_s, wd_s, sem, *, layer):
    i = pl.program_id(0)
    used = i < nu_ref[0]
    expert = be_ref[i]
    first = jnp.logical_or(i == 0, expert != be_ref[jnp.maximum(i - 1, 0)])

    def fetch(e):
        return (pltpu.make_async_copy(wg_hbm.at[layer, e], wg_st, sem.at[0]),
                pltpu.make_async_copy(wu_hbm.at[layer, e], wu_st, sem.at[1]),
                pltpu.make_async_copy(wd_hbm.at[layer, e], wd_st, sem.at[2]))

    @pl.when(i == 0)
    def _():
        for c in fetch(expert):
            c.start()

    @pl.when(jnp.logical_and(used, first))
    def _():
        for c in fetch(expert):
            c.wait()
        wgu_s[:, :D_EXPERT] = wg_st[...].astype(BF16)
        wgu_s[:, D_EXPERT:] = wu_st[...].astype(BF16)
        wd_s[...] = wd_st[...].astype(BF16)
        nxt = nx_ref[i]

        @pl.when(nxt >= 0)
        def _():
            for c in fetch(nxt):
                c.start()

    @pl.when(used)
    def _():
        sub = MOE_BLOCK // EXPERT_SPLIT
        hs = []
        for r in range(EXPERT_SPLIT):
            lo, hi = _unpack_rows(x_ref[r * sub:(r + 1) * sub, :])
            x = jnp.concatenate([lo, hi], axis=1).astype(BF16)
            hs.append(jnp.dot(x, wgu_s[...], preferred_element_type=F32))
        for r, h in enumerate(hs):
            hid = (_silu(h[:, :D_EXPERT]) * h[:, D_EXPERT:]).astype(BF16)
            y_ref[r * sub:(r + 1) * sub, :] = _pack_rows(jnp.dot(hid, wd_s[...], preferred_element_type=F32))

    @pl.when(jnp.logical_not(used))
    def _():
        y_ref[...] = jnp.zeros_like(y_ref)


def expert_blocks(xg, blk_e, blk_next, n_used, w_gate, w_up, w_down, layer):
    n_rows, half = xg.shape
    d = 2 * half
    n_blocks = n_rows // MOE_BLOCK

    def xmap(i, be, nx, nu):
        return (jnp.minimum(i, nu[0] - 1), 0)

    hbm = pl.BlockSpec(memory_space=pl.ANY)
    grid_spec = pltpu.PrefetchScalarGridSpec(
        num_scalar_prefetch=3,
        grid=(n_blocks,),
        in_specs=[pl.BlockSpec((MOE_BLOCK, half), xmap), hbm, hbm, hbm],
        out_specs=pl.BlockSpec((MOE_BLOCK, half), lambda i, be, nx, nu: (i, 0)),
        scratch_shapes=[pltpu.VMEM((d, D_EXPERT), F32), pltpu.VMEM((d, D_EXPERT), F32),
                        pltpu.VMEM((D_EXPERT, d), F32),
                        pltpu.VMEM((d, 2 * D_EXPERT), BF16), pltpu.VMEM((D_EXPERT, d), BF16),
                        pltpu.SemaphoreType.DMA((3,))],
    )
    return pl.pallas_call(
        functools.partial(_expert_kernel, layer=layer),
        grid_spec=grid_spec,
        out_shape=jax.ShapeDtypeStruct((n_rows, half), U32),
        compiler_params=_params("arbitrary"),
        name="expert_blocks",
    )(blk_e, blk_next, n_used, xg, w_gate, w_up, w_down)


def _dest_kernel(ps_ref, idx_ref, rank_ref, dest_ref):
    idx = idx_ref[...]
    dest = rank_ref[...]
    for e in range(N_EXPERTS):
        dest = dest + jnp.where(idx == e, ps_ref[e], 0)
    dest_ref[...] = dest


def dispatch_rows(idx, rank, pstarts, *, tn=2048):
    k, n = idx.shape
    tn = min(tn, n)
    spec = pl.BlockSpec((k, tn), lambda i, ps: (0, i))
    return pl.pallas_call(
        _dest_kernel,
        grid_spec=pltpu.PrefetchScalarGridSpec(num_scalar_prefetch=1, grid=(n // tn,),
                                               in_specs=[spec, spec], out_specs=spec),
        out_shape=jax.ShapeDtypeStruct((k, n), I32),
        compiler_params=_params("arbitrary"),
        name="dispatch_rows",
    )(pstarts, idx, rank)


def _combine_kernel(dest_ref, x_ref, xb_ref, gw_ref, sg_ref, su_ref, sd_ref, g_ref, b_ref, yg_ref,
                    y_ref, yb_ref, yp_ref, buf_a, buf_b, sem, *, n_tok):
    i = pl.program_id(0)
    n_steps = pl.num_programs(0)
    tm = x_ref.shape[0] // 2
    half = x_ref.shape[1] // 2
    bufs = (buf_a, buf_b)

    def row_copy(tile, which, grp, s, k):
        d = dest_ref[k * n_tok + tile * tm + grp * SUBLANE + s]
        return pltpu.make_async_copy(yg_ref.at[pl.ds(d, 1)], bufs[which].at[k, grp, pl.ds(s, 1)], sem.at[which])

    def wait_tile(which):
        for k in range(TOP_K):
            pltpu.make_async_copy(yg_ref.at[pl.ds(0, tm)], yp_ref.at[pl.ds(0, tm)], sem.at[which]).wait()

    @pl.when(i == 0)
    def _():
        def issue(grp, c):
            for s in range(SUBLANE):
                for k in range(TOP_K):
                    row_copy(0, 0, grp, s, k).start(priority=k % 2)
            return c
        lax.fori_loop(0, tm // SUBLANE, issue, 0)

    for which in range(2):
        rows = slice(which * tm, (which + 1) * tm)
        wait_tile(which)
        nxt = jnp.minimum(2 * i + which + 1, 2 * n_steps - 1)
        for grp in range(tm // SUBLANE):
            for s in range(SUBLANE):
                for k in range(TOP_K):
                    row_copy(nxt, 1 - which, grp, s, k).start(priority=k % 2)

        xb = xb_ref[rows, :]
        hg = jnp.dot(xb, sg_ref[...], preferred_element_type=F32)
        hu = jnp.dot(xb, su_ref[...], preferred_element_type=F32)
        hid = (_silu(hg) * hu).astype(BF16)
        f = jnp.dot(hid, sd_ref[...], preferred_element_type=F32)

        gw = gw_ref[rows, :]
        lo = jnp.zeros((tm, half), F32)
        hi = jnp.zeros((tm, half), F32)
        for k in range(TOP_K):
            l_k, h_k = _unpack_rows(bufs[which][k].reshape(tm, half))
            lo = lo + gw[:, k:k + 1] * l_k
            hi = hi + gw[:, k:k + 1] * h_k
        f = f + jnp.concatenate([lo, hi], axis=1)
        y = _layer_norm_rows(DEEPNORM_ALPHA * x_ref[rows, :] + f, g_ref[...], b_ref[...])
        y_ref[rows, :] = y
        yb_ref[rows, :] = y.astype(BF16)
        yp_ref[rows, :] = _pack_rows(y)

    @pl.when(i == n_steps - 1)
    def _():
        wait_tile(0)


def moe_combine(x, xb, gw, dest, yg, s_gate, s_up, s_down, g, b):
    n, d = x.shape
    tm = min(COMBINE_TILE, n // 2)
    once = pl.Buffered(1)
    row = pl.BlockSpec((2 * tm, d), lambda i, *_: (i, 0))
    prow = pl.BlockSpec((2 * tm, d // 2), lambda i, *_: (i, 0))
    vec = pl.BlockSpec((1, d), lambda i, *_: (0, 0), pipeline_mode=once)
    gather_buf = pltpu.VMEM((TOP_K, tm // SUBLANE, SUBLANE, d // 2), U32)
    grid_spec = pltpu.PrefetchScalarGridSpec(
        num_scalar_prefetch=1,
        grid=(n // (2 * tm),),
        in_specs=[row, row, pl.BlockSpec((2 * tm, TOP_K), lambda i, *_: (i, 0)),
                  pl.BlockSpec((d, D_EXPERT), lambda i, *_: (0, 0), pipeline_mode=once),
                  pl.BlockSpec((d, D_EXPERT), lambda i, *_: (0, 0), pipeline_mode=once),
                  pl.BlockSpec((D_EXPERT, d), lambda i, *_: (0, 0), pipeline_mode=once),
                  vec, vec, pl.BlockSpec(memory_space=pl.ANY)],
        out_specs=[row, row, prow],
        scratch_shapes=[gather_buf, gather_buf, pltpu.SemaphoreType.DMA((2,))],
    )
    return pl.pallas_call(
        functools.partial(_combine_kernel, n_tok=n),
        grid_spec=grid_spec,
        out_shape=[jax.ShapeDtypeStruct((n, d), F32), jax.ShapeDtypeStruct((n, d), BF16),
                   jax.ShapeDtypeStruct((n, d // 2), U32)],
        compiler_params=_params("arbitrary"),
        name="moe_combine",
    )(dest, x, xb, gw, s_gate, s_up, s_down, g.reshape(1, d), b.reshape(1, d), yg)


def moe_ffn(x, xb, xp, layer, w_router, r_bias, w_gate, w_up, w_down, s_gate, s_up, s_down, ln_g, ln_b):
    n_tok = x.shape[0]
    idx, rank, gw, counts = router(x, w_router, r_bias)
    counts = counts.reshape(N_EXPERTS)
    pcounts = (counts + MOE_BLOCK - 1) // MOE_BLOCK * MOE_BLOCK
    pends = jnp.cumsum(pcounts).astype(I32)
    pstarts = pends - pcounts
    n_blocks = n_tok * TOP_K // MOE_BLOCK + N_EXPERTS
    blk_start = jnp.arange(n_blocks, dtype=I32) * MOE_BLOCK
    blk_e = jnp.minimum(jnp.sum(pends[None, :] <= blk_start[:, None], axis=1), N_EXPERTS - 1).astype(I32)
    n_used = (pends[-1:] // MOE_BLOCK).astype(I32)
    eids = jnp.arange(N_EXPERTS, dtype=I32)
    later = jnp.where((eids[None, :] > eids[:, None]) & (pcounts[None, :] > 0), eids[None, :], N_EXPERTS)
    nxt_e = jnp.min(later, axis=1)
    nxt_e = jnp.where(nxt_e == N_EXPERTS, -1, nxt_e)
    blk_next = jnp.sum(jnp.where(blk_e[:, None] == eids[None, :], nxt_e[None, :], 0), axis=1).astype(I32)
    dest = dispatch_rows(idx, rank, pstarts.astype(I32)).reshape(TOP_K * n_tok)

    xg = dispatch(xp, dest, pends, pcounts.astype(I32), n_rows=n_blocks * MOE_BLOCK)
    yg = expert_blocks(xg, blk_e, blk_next, n_used, w_gate, w_up, w_down, layer)
    return moe_combine(x, xb, gw.T, dest, yg, s_gate, s_up, s_down, ln_g, ln_b)


def kernel(x, ln_g, ln_b, ev_w_in, ev_decay_exp, ev_gn_w, ev_rpb, ev_w_out, od_w_in, od_lb, od_norm_w, od_w_out, moe_w_router, moe_bias, moe_w_gate, moe_w_up, moe_w_down, sh_w_gate, sh_w_up, sh_w_down):
    batch, seq, d = x.shape
    n = batch * seq
    xf = x.reshape(n, d).astype(F32)
    xb = xf.astype(BF16)
    for layer in range(DEPTH):
        j = layer // 2
        if layer % 2 == 0:
            ph = matmul(xb, ev_w_in, j, out_dtype=BF16, heads=True)
            o_ret = retention(ph, ev_decay_exp[j], ev_gn_w[j].astype(F32), batch=batch, seq=seq)
            o_na = neighborhood_attention(ph, ev_rpb[j], batch=batch, seq=seq, slab0=4 * RET_W // LANE)
            mixed = jnp.concatenate([o_ret, o_na], axis=1)
            w_out = ev_w_out
        else:
            ph = matmul(xb, od_w_in, j, out_dtype=BF16, heads=True)
            mixed = hgrn2(ph, od_lb, od_norm_w[j].astype(F32), layer=layer, batch=batch, seq=seq)
            w_out = od_w_out
        m = matmul(mixed, w_out, j, out_dtype=F32, heads=False)
        xf, xb, xp = ln_residual(xf, m, ln_g[layer, 0].astype(F32), ln_b[layer, 0].astype(F32))
        xf, xb, xp = moe_ffn(xf, xb, xp, layer, moe_w_router[layer].astype(F32), moe_bias[layer],
                             moe_w_gate, moe_w_up, moe_w_down, sh_w_gate[layer].astype(BF16),
                             sh_w_up[layer].astype(BF16), sh_w_down[layer].astype(BF16),
                             ln_g[layer, 1].astype(F32), ln_b[layer, 1].astype(F32))
    return xf.reshape(batch, seq, d)
```

```python
import functools

import numpy as np
import jax
import jax.numpy as jnp
from jax import lax
from jax.experimental import pallas as pl
from jax.experimental.pallas import tpu as pltpu

F32 = jnp.float32
BF16 = jnp.bfloat16
U32 = jnp.uint32
I32 = jnp.int32

LANE = 128
SUBLANE = 8
VMEM_LIMIT = 56 * 1024 * 1024

DEPTH = 2
GRID_W = 64
RET_HEADS = 8
RET_DK = 256
RET_W = RET_HEADS * RET_DK
RET_CHUNK = 128
ROPE_BASE = 10000.0
NA_HEADS = 16
NA_DH = 128
NA_W = NA_HEADS * NA_DH
NA_KH = 8
NA_KW = 16
NA_UNROLL = 32
HG_HEADS = 32
HG_DK = 128
HG_W = HG_HEADS * HG_DK
HG_CHUNK = 128
HG_SUB = SUBLANE
HG_ROWS = 1024
HG_UNROLL = 8
N_EXPERTS = 64
TOP_K = 8
N_GROUPS = 8
TOPK_GROUPS = 4
E_PER_GROUP = N_EXPERTS // N_GROUPS
D_EXPERT = 384
ROUTED_SCALE = 2.5
MOE_BLOCK = 512
EXPERT_SPLIT = 2
ROUTER_TILE = 512
DISPATCH_TILE = 512
COMBINE_TILE = 128
ISSUE_UNROLL = 8
DEEPNORM_ALPHA = (2.0 * DEPTH) ** 0.25
LN_EPS = 1e-5
NORM_EPS = 1e-6
MASK_NEG = -1e30
LOG2_E = 1.4426950408889634


def _params(*sem):
    return pltpu.CompilerParams(dimension_semantics=sem, vmem_limit_bytes=VMEM_LIMIT)


def _sigmoid_pair(z):
    e = jnp.exp(-jnp.abs(z))
    r = 1.0 / (1.0 + e)
    er = e * r
    pos = z >= 0
    return jnp.where(pos, r, er), jnp.where(pos, er, r)


def _silu(z):
    return z * (0.5 * jnp.tanh(0.5 * z) + 0.5)


def _pack_rows(y):
    half = y.shape[1] // 2
    lo = lax.bitcast_convert_type(y[:, :half].astype(BF16).astype(F32), U32) >> 16
    hi = lax.bitcast_convert_type(y[:, half:].astype(BF16).astype(F32), U32) & jnp.uint32(0xFFFF0000)
    return hi | lo


def _unpack_rows(u):
    lo = lax.bitcast_convert_type(u << 16, F32)
    hi = lax.bitcast_convert_type(u & jnp.uint32(0xFFFF0000), F32)
    return lo, hi


def _mm_kernel(a_ref, w_ref, o_ref, wb_ref, *, heads):
    @pl.when(pl.program_id(1) == 0)
    def _():
        wb_ref[...] = w_ref[...].astype(BF16)

    acc = jnp.dot(a_ref[...], wb_ref[...], preferred_element_type=F32)
    if heads:
        for j in range(o_ref.shape[0]):
            o_ref[j] = acc[:, j * LANE:(j + 1) * LANE].astype(o_ref.dtype)
    else:
        o_ref[...] = acc.astype(o_ref.dtype)


def matmul(a, w, layer, *, out_dtype, heads, tm=1024, tn=512):
    n, k = a.shape
    m = w.shape[2]
    tm, tn = min(tm, n), min(tn, m)
    if heads:
        out_shape = jax.ShapeDtypeStruct((m // LANE, n, LANE), out_dtype)
        out_spec = pl.BlockSpec((tn // LANE, tm, LANE), lambda j, i: (j, i, 0))
    else:
        out_shape = jax.ShapeDtypeStruct((n, m), out_dtype)
        out_spec = pl.BlockSpec((tm, tn), lambda j, i: (i, j))
    return pl.pallas_call(
        functools.partial(_mm_kernel, heads=heads),
        grid=(m // tn, n // tm),
        in_specs=[pl.BlockSpec((tm, k), lambda j, i: (i, 0)),
                  pl.BlockSpec((None, k, tn), lambda j, i: (layer, 0, j))],
        out_specs=out_spec,
        out_shape=out_shape,
        scratch_shapes=[pltpu.VMEM((k, tn), BF16)],
        compiler_params=_params("arbitrary", "arbitrary"),
        name="proj_matmul",
    )(a, w)


def _layer_norm_rows(z, g, b):
    mu = jnp.mean(z, axis=-1, keepdims=True)
    zc = z - mu
    var = jnp.mean(zc * zc, axis=-1, keepdims=True)
    return zc * lax.rsqrt(var + LN_EPS) * g + b


def _ln_kernel(x_ref, m_ref, g_ref, b_ref, y_ref, yb_ref, yp_ref):
    z = DEEPNORM_ALPHA * x_ref[...] + m_ref[...].astype(F32)
    y = _layer_norm_rows(z, g_ref[...], b_ref[...])
    y_ref[...] = y
    yb_ref[...] = y.astype(BF16)
    yp_ref[...] = _pack_rows(y)


def ln_residual(x, m, g, b, *, tm=256):
    n, d = x.shape
    tm = min(tm, n)
    row = pl.BlockSpec((tm, d), lambda i: (i, 0))
    prow = pl.BlockSpec((tm, d // 2), lambda i: (i, 0))
    vec = pl.BlockSpec((1, d), lambda i: (0, 0))
    return pl.pallas_call(
        _ln_kernel,
        grid=(n // tm,),
        in_specs=[row, row, vec, vec],
        out_specs=[row, row, prow],
        out_shape=[jax.ShapeDtypeStruct((n, d), F32), jax.ShapeDtypeStruct((n, d), BF16),
                   jax.ShapeDtypeStruct((n, d // 2), U32)],
        compiler_params=_params("arbitrary"),
        name="ln_residual",
    )(x, m, g.reshape(1, d), b.reshape(1, d))


def _ret_kernel(lg_ref, q_ref, k_ref, v_ref, cos_ref, sin_ref, *rest, rev, n_chunks):
    if rev:
        oa_ref, g_ref, gn_ref, out_ref, s_ref = rest
    else:
        out_ref, s_ref = rest
    C = RET_CHUNK
    h = pl.program_id(1)

    @pl.when(pl.program_id(2) == 0)
    def _():
        s_ref[...] = jnp.zeros_like(s_ref)

    lg = lg_ref[1 if rev else 0, h]
    row = lax.broadcasted_iota(I32, (C, RET_DK), 0).astype(F32)
    if rev:
        q_dec = jnp.exp(lg * (C - row))
        k_dec = jnp.exp(lg * row)
    else:
        q_dec = jnp.exp(lg * (row + 1.0))
        k_dec = jnp.exp(lg * (C - 1.0 - row))
        dist = (lax.broadcasted_iota(I32, (C, C), 0)
                - lax.broadcasted_iota(I32, (C, C), 1)).astype(F32)
        decay = jnp.where(dist >= 0,
                          jnp.exp(lg_ref[0, h] * jnp.maximum(dist, 0.0)),
                          jnp.exp(lg_ref[1, h] * jnp.maximum(-dist, 0.0)))
    chunk_decay = jnp.exp(jnp.full((1, 1), C, F32) * lg)

    def rot(ref, sl, cos, sin):
        t1 = ref[0, sl, :].astype(F32)
        t2 = ref[1, sl, :].astype(F32)
        return jnp.concatenate([t1 * cos - t2 * sin, t1 * sin + t2 * cos], axis=1)

    def local_part(sl):
        cos, sin = cos_ref[sl, :], sin_ref[sl, :]
        qr = rot(q_ref, sl, cos, sin)
        kr = rot(k_ref, sl, cos, sin) * (RET_DK ** -0.5)
        v = jnp.concatenate([v_ref[0, sl, :], v_ref[1, sl, :]], axis=1)
        kv = lax.dot_general((kr * k_dec).astype(BF16), v, (((0,), (0,)), ((), ())),
                             preferred_element_type=F32)
        if rev:
            o = jnp.concatenate([oa_ref[0, sl, :], oa_ref[1, sl, :]], axis=1)
        else:
            scores = lax.dot_general(qr.astype(BF16), kr.astype(BF16), (((1,), (1,)), ((), ())),
                                     preferred_element_type=F32) * decay
            o = jnp.dot(scores.astype(BF16), v, preferred_element_type=F32)
        return o, (qr * q_dec).astype(BF16), kv

    def carried_part(sl, local):
        o, q_in, kv = local
        state = s_ref[...]
        o = o + jnp.dot(q_in, state.astype(BF16), preferred_element_type=F32)
        s_ref[...] = chunk_decay * state + kv
        if rev:
            mu = jnp.mean(o, axis=-1, keepdims=True)
            oc = o - mu
            var = jnp.mean(oc * oc, axis=-1, keepdims=True)
            gate = jnp.concatenate([g_ref[0, sl, :], g_ref[1, sl, :]], axis=1).astype(F32)
            res = oc * lax.rsqrt(var + NORM_EPS) * gn_ref[...] * _silu(gate)
            out_ref[sl, :] = res.astype(out_ref.dtype)
        else:
            out_ref[0, sl, :] = o[:, :LANE]
            out_ref[1, sl, :] = o[:, LANE:]

    order = range(n_chunks - 1, -1, -1) if rev else range(n_chunks)
    slices = [slice(ci * C, (ci + 1) * C) for ci in order]
    locals_ = [local_part(sl) for sl in slices]
    for sl, local in zip(slices, locals_):
        carried_part(sl, local)


def _ret_pass(lg, ph, cos, sin, extra, *, rev, batch, seq, rows):
    n_blk = seq // rows
    n_chunks = rows // RET_CHUNK
    H = RET_HEADS

    def blk(step):
        return (n_blk - 1 - step) if rev else step

    def slab(group):
        return pl.BlockSpec((2, rows, LANE), lambda b, h, s: (group * H + h, b * n_blk + blk(s), 0))

    tab = pl.BlockSpec((rows, LANE), lambda b, h, s: (blk(s), 0))
    in_specs = [pl.BlockSpec(memory_space=pltpu.SMEM), slab(0), slab(1), slab(2), tab, tab]
    args = [lg, ph, ph, ph, cos, sin]
    if rev:
        oa, gn_w = extra
        in_specs += [pl.BlockSpec((2, rows, LANE), lambda b, h, s: (h, b * n_blk + blk(s), 0)),
                     slab(3),
                     pl.BlockSpec((1, RET_DK), lambda b, h, s: (0, h))]
        args += [oa, ph, gn_w.reshape(1, RET_W)]
        out_shape = jax.ShapeDtypeStruct((batch * seq, RET_W), BF16)
        out_spec = pl.BlockSpec((rows, RET_DK), lambda b, h, s: (b * n_blk + blk(s), h))
    else:
        out_shape = jax.ShapeDtypeStruct((2 * H, batch * seq, LANE), F32)
        out_spec = pl.BlockSpec((2, rows, LANE), lambda b, h, s: (h, b * n_blk + blk(s), 0))
    return pl.pallas_call(
        functools.partial(_ret_kernel, rev=rev, n_chunks=n_chunks),
        grid=(batch, H, n_blk),
        in_specs=in_specs,
        out_specs=out_spec,
        out_shape=out_shape,
        scratch_shapes=[pltpu.VMEM((RET_DK, RET_DK), F32)],
        compiler_params=_params("arbitrary", "arbitrary", "arbitrary"),
        name="retention_bwd" if rev else "retention_fwd",
    )(*args)


def retention(ph, decay_exp, gn_w, *, batch, seq, rows=1024):
    rows = min(rows, seq)
    log_g = jnp.log1p(-jnp.exp2(-decay_exp.astype(F32)))
    inv = ROPE_BASE ** (-jnp.linspace(0.0, 1.0, RET_DK // 2, dtype=F32))
    ang = jnp.arange(seq, dtype=F32)[:, None] * inv[None, :]
    cos, sin = jnp.cos(ang), jnp.sin(ang)
    o_a = _ret_pass(log_g, ph, cos, sin, None, rev=False, batch=batch, seq=seq, rows=rows)
    return _ret_pass(log_g, ph, cos, sin, (o_a, gn_w), rev=True, batch=batch, seq=seq, rows=rows)


def _na_kernel(q_ref, k_ref, v_ref, tab_ref, o_ref, *, grid_rows):
    span = NA_KH * GRID_W
    scale = NA_DH ** -0.5

    def body(it, carry):
        rows = [it * NA_UNROLL + j for j in range(NA_UNROLL)]
        starts = [jnp.clip(r - NA_KH // 2, 0, grid_rows - NA_KH) for r in rows]
        scores = []
        for r, r0 in zip(rows, starts):
            q = q_ref[0, pl.ds(pl.multiple_of(r * GRID_W, GRID_W), GRID_W), :]
            kw = k_ref[0, pl.ds(pl.multiple_of(r0 * GRID_W, GRID_W), span), :]
            s = lax.dot_general(q, kw, (((1,), (1,)), ((), ())), preferred_element_type=F32)
            scores.append(s * scale + tab_ref[0, r0 - r + NA_KH - 1])
        probs = []
        for s in scores:
            p = jnp.exp(s - jnp.max(s, axis=-1, keepdims=True))
            probs.append((p.astype(BF16), jnp.sum(p, axis=-1, keepdims=True)))
        for r, r0, (p, l) in zip(rows, starts, probs):
            vw = v_ref[0, pl.ds(pl.multiple_of(r0 * GRID_W, GRID_W), span), :]
            o = jnp.dot(p, vw, preferred_element_type=F32) / l
            o_ref[pl.ds(pl.multiple_of(r * GRID_W, GRID_W), GRID_W), :] = o.astype(o_ref.dtype)
        return carry

    lax.fori_loop(0, grid_rows // NA_UNROLL, body, 0)


def _na_bias_table(rpb):
    qc = np.arange(GRID_W)[:, None]
    kc = np.arange(GRID_W)[None, :]
    wstart = np.clip(qc - NA_KW // 2, 0, GRID_W - NA_KW)
    valid = (kc >= wstart) & (kc < wstart + NA_KW)
    dc = np.clip(kc - qc, -(NA_KW - 1), NA_KW - 1) + NA_KW - 1
    onehot = ((dc[None] == np.arange(2 * NA_KW - 1)[:, None, None]) & valid[None]).astype(np.float32)
    by_col = jnp.einsum('hrj,jck->hrck', rpb, jnp.asarray(onehot), precision=lax.Precision.HIGHEST)
    by_col = by_col + jnp.asarray(np.where(valid, 0.0, MASK_NEG).astype(np.float32))[None, None]
    tab = jnp.stack([by_col[:, off:off + NA_KH] for off in range(NA_KH)], axis=1)
    return tab.transpose(0, 1, 3, 2, 4).reshape(NA_HEADS, NA_KH, GRID_W, NA_KH * GRID_W)


def neighborhood_attention(ph, rpb, *, batch, seq, slab0):
    grid_rows = seq // GRID_W
    assert grid_rows >= NA_KH and grid_rows % NA_UNROLL == 0
    tab = _na_bias_table(rpb.astype(F32))

    def slab(group):
        return pl.BlockSpec((1, seq, LANE), lambda b, h: (slab0 + group * NA_HEADS + h, b, 0))

    return pl.pallas_call(
        functools.partial(_na_kernel, grid_rows=grid_rows),
        grid=(batch, NA_HEADS),
        in_specs=[slab(0), slab(1), slab(2),
                  pl.BlockSpec((1, NA_KH, GRID_W, NA_KH * GRID_W), lambda b, h: (h, 0, 0, 0))],
        out_specs=pl.BlockSpec((seq, NA_DH), lambda b, h: (b, h)),
        out_shape=jax.ShapeDtypeStruct((batch * seq, NA_W), BF16),
        compiler_params=_params("arbitrary", "arbitrary"),
        name="neighborhood_attention",
    )(ph, ph, ph, tab)


def _hg_widths():
    widths, w = [], HG_CHUNK // 2
    while w >= HG_SUB:
        widths.append(w)
        w //= 2
    return widths


def _hg_kernel(lb_ref, zq_ref, zf_ref, zi_ref, *rest, rev, layer, n_chunks):
    if rev:
        of_ref, zg_ref, nw_ref, out_ref, st_ref, lvl_ref, dg_ref, bq_ref = rest
    else:
        out_ref, st_ref, lvl_ref, dg_ref, bq_ref = rest
    C = HG_CHUNK
    U = HG_SUB
    widths = _hg_widths()

    @pl.when(pl.program_id(2) == 0)
    def _():
        st_ref[...] = jnp.zeros_like(st_ref)
        s_i = lax.broadcasted_iota(I32, (C, C), 0)
        t_i = lax.broadcasted_iota(I32, (C, C), 1)
        earlier = (s_i > t_i) if rev else (s_i < t_i)
        split = jnp.where(earlier, s_i ^ t_i, 0)
        for l, w in enumerate(widths):
            lvl_ref[l] = jnp.where(split >= w, jnp.where(split < 2 * w, 1.0, 0.0), 0.0)
        near = jnp.where((s_i >= t_i) if rev else (s_i <= t_i), jnp.where((s_i ^ t_i) < U, 1.0, 0.0), 0.0)
        for tt in range(U):
            dg_ref[tt] = near * jnp.where((t_i & (U - 1)) == tt, 1.0, 0.0)

    lbp = lb_ref[...]
    pe = jnp.exp(lbp - jnp.max(lbp, axis=0, keepdims=True))
    share = pe / jnp.sum(pe, axis=0, keepdims=True)
    lb = jnp.sum(share[:layer + 1], axis=0, keepdims=True) - share[0:1]
    one_m_lb = 1.0 - lb

    ri = lax.broadcasted_iota(I32, (C, C), 0)
    ci_ = lax.broadcasted_iota(I32, (C, C), 1)
    tri = jnp.where((ri <= ci_) if rev else (ri >= ci_), 1.0, 0.0).astype(BF16)
    row = lax.broadcasted_iota(I32, (C, HG_DK), 0)

    def rows_of(step):
        cidx = (n_chunks - 1 - step) if rev else step
        return pl.ds(pl.multiple_of(cidx * C, C), C)

    def local_part(sl, slot):
        q = _silu(zq_ref[0, sl, :].astype(F32))
        sig, nsig = _sigmoid_pair(zf_ref[0, sl, :].astype(F32))
        k = one_m_lb * nsig
        lf = jnp.log(lb + one_m_lb * sig) * LOG2_E
        v = zi_ref[0, sl, :]

        hi = lf.astype(BF16)
        r1 = lf - hi.astype(F32)
        mid = r1.astype(BF16)
        lo = (r1 - mid.astype(F32)).astype(BF16)
        parts = jnp.dot(tri, jnp.concatenate([hi, mid, lo], axis=1), preferred_element_type=F32)
        b = parts[:, :LANE] + parts[:, LANE:2 * LANE] + parts[:, 2 * LANE:]

        att_t = jnp.zeros((C, C), F32)
        for l, w in enumerate(widths):
            b3 = b.reshape(C // (2 * w), 2 * w, HG_DK)
            edge = b3[:, w:w + 1, :] if rev else b3[:, w - 1:w, :]
            d = (b3 - edge).reshape(C, HG_DK)
            is_q = ((row & w) == 0) if rev else ((row & w) != 0)
            x = (jnp.where(is_q, q, k) * jnp.exp2(jnp.where(is_q, d, -d))).astype(BF16)
            a = lax.dot_general(x, x, (((1,), (1,)), ((), ())), preferred_element_type=F32)
            att_t = att_t + a * lvl_ref[l]

        bq_ref[slot, 0] = b
        bq_ref[slot, 1] = q

        def step_rows(which, tt):
            return jnp.concatenate(
                [jnp.broadcast_to(bq_ref[slot, which, pl.ds(n * U + tt, 1), :], (U, HG_DK))
                 for n in range(C // U)], axis=0)

        for tt in range(U):
            e = jnp.exp2(jnp.minimum(step_rows(0, tt) - b, 0.0))
            a = jnp.sum(e * k * step_rows(1, tt), axis=-1, keepdims=True)
            att_t = att_t + a * dg_ref[tt]
        o = lax.dot_general(att_t.astype(BF16), v, (((0,), (0,)), ((), ())), preferred_element_type=F32)

        b_end = b[0:1, :] if rev else b[C - 1:C, :]
        kd = (k * jnp.exp2(b_end - b)).astype(BF16)
        kv = lax.dot_general(v, kd, (((0,), (0,)), ((), ())), preferred_element_type=F32)
        return o, (q * jnp.exp2(b)).astype(BF16), kv, jnp.exp2(b_end)

    def carried_part(sl, local):
        o, q_dec, kv, end_decay = local
        state_t = st_ref[...]
        o = o + lax.dot_general(q_dec, state_t.astype(BF16), (((1,), (1,)), ((), ())),
                                preferred_element_type=F32)
        st_ref[...] = state_t * end_decay + kv
        if rev:
            o = o + of_ref[0, sl, :]
            o = o * lax.rsqrt(jnp.mean(o * o, axis=-1, keepdims=True) + NORM_EPS)
            res = o * nw_ref[...] * _silu(zg_ref[0, sl, :].astype(F32))
            out_ref[sl, :] = res.astype(out_ref.dtype)
        else:
            out_ref[0, sl, :] = o

    def group(it, carry):
        slices = [rows_of(it * HG_UNROLL + j) for j in range(HG_UNROLL)]
        locals_ = [local_part(sl, slot) for slot, sl in enumerate(slices)]
        for sl, local in zip(slices, locals_):
            carried_part(sl, local)
        return carry

    lax.fori_loop(0, n_chunks // HG_UNROLL, group, 0)


def _hg_pass(od_lb, ph, extra, *, rev, layer, batch, seq, rows):
    n_blk = seq // rows
    H = HG_HEADS

    def blk(step):
        return (n_blk - 1 - step) if rev else step

    def slab(group):
        return pl.BlockSpec((1, rows, LANE), lambda b, h, s: (group * H + h, b * n_blk + blk(s), 0))

    in_specs = [pl.BlockSpec((od_lb.shape[0], HG_DK), lambda b, h, s: (0, h)),
                slab(0), slab(2 if rev else 1), slab(3)]
    args = [od_lb, ph, ph, ph]
    if rev:
        o_f, norm_w = extra
        in_specs += [pl.BlockSpec((1, rows, LANE), lambda b, h, s: (h, b * n_blk + blk(s), 0)),
                     slab(4),
                     pl.BlockSpec((1, HG_DK), lambda b, h, s: (0, h))]
        args += [o_f, ph, norm_w.reshape(1, HG_W)]
        out_shape = jax.ShapeDtypeStruct((batch * seq, HG_W), BF16)
        out_spec = pl.BlockSpec((rows, HG_DK), lambda b, h, s: (b * n_blk + blk(s), h))
    else:
        out_shape = jax.ShapeDtypeStruct((H, batch * seq, LANE), F32)
        out_spec = pl.BlockSpec((1, rows, LANE), lambda b, h, s: (h, b * n_blk + blk(s), 0))
    return pl.pallas_call(
        functools.partial(_hg_kernel, rev=rev, layer=layer, n_chunks=rows // HG_CHUNK),
        grid=(batch, H, n_blk),
        in_specs=in_specs,
        out_specs=out_spec,
        out_shape=out_shape,
        scratch_shapes=[pltpu.VMEM((HG_DK, HG_DK), F32),
                        pltpu.VMEM((len(_hg_widths()), HG_CHUNK, HG_CHUNK), F32),
                        pltpu.VMEM((HG_SUB, HG_CHUNK, HG_CHUNK), F32),
                        pltpu.VMEM((HG_UNROLL, 2, HG_CHUNK, HG_DK), F32)],
        compiler_params=_params("arbitrary", "arbitrary", "arbitrary"),
        name="hgrn2_bwd" if rev else "hgrn2_fwd",
    )(*args)


def hgrn2(ph, od_lb, norm_w, *, layer, batch, seq):
    rows = min(HG_ROWS, seq)
    od_lb = od_lb.astype(F32)
    o_f = _hg_pass(od_lb, ph, None, rev=False, layer=layer, batch=batch, seq=seq, rows=rows)
    return _hg_pass(od_lb, ph, (o_f, norm_w), rev=True, layer=layer, batch=batch, seq=seq, rows=rows)


def _router_kernel(wt_ref, bias_ref, tri_ref, x_ref, idx_ref, rank_ref, gw_ref, cnt_ref, run_ref):
    tm = x_ref.shape[0]
    G, P = N_GROUPS, E_PER_GROUP

    @pl.when(pl.program_id(0) == 0)
    def _():
        run_ref[...] = jnp.zeros_like(run_ref)

    def split(a):
        hi = a.astype(BF16)
        return hi, (a - hi.astype(F32)).astype(BF16)

    def nt_dot(a, b):
        return lax.dot_general(a, b, (((1,), (1,)), ((), ())), preferred_element_type=F32)

    w_hi, w_lo = split(wt_ref[...])
    x_hi, x_lo = split(x_ref[...])
    logits = nt_dot(w_hi, x_hi) + (nt_dot(w_lo, x_hi) + nt_dot(w_hi, x_lo))
    scores = _sigmoid_pair(logits)[0]
    sc3 = scores.reshape(G, P, tm)
    x3 = (scores + bias_ref[...]).reshape(G, P, tm)
    neg = -jnp.inf

    def max01(a):
        return jnp.max(jnp.max(a, axis=0, keepdims=True), axis=1, keepdims=True)

    def min01(a):
        return jnp.min(jnp.min(a, axis=0, keepdims=True), axis=1, keepdims=True)

    def sum01(a):
        return jnp.sum(jnp.sum(a, axis=0, keepdims=True), axis=1, keepdims=True)

    pidx = lax.broadcasted_iota(I32, (G, P, tm), 1)
    m1 = jnp.max(x3, axis=1, keepdims=True)
    i1 = jnp.min(jnp.where(x3 == m1, pidx, P), axis=1, keepdims=True)
    m2 = jnp.max(jnp.where(pidx == i1, neg, x3), axis=1, keepdims=True)
    gs = m1 + m2
    gidx = lax.broadcasted_iota(I32, (G, 1, tm), 0)
    before = jnp.zeros((G, 1, tm), I32)
    for g in range(G):
        other = gs[g:g + 1]
        ahead = jnp.where(other > gs, 1, jnp.where(other == gs, jnp.where(gidx > g, 1, 0), 0))
        before = before + ahead
    xm = jnp.where(before < TOPK_GROUPS, x3, neg)

    eidx = lax.broadcasted_iota(I32, (G, P, tm), 0) * P + pidx
    picks = []
    sel = jnp.zeros((G, P, tm), F32)
    for _ in range(TOP_K):
        m = max01(xm)
        ik = min01(jnp.where(xm == m, eidx, N_EXPERTS))
        hit = eidx == ik
        xm = jnp.where(hit, neg, xm)
        sel = jnp.where(hit, 1.0, sel)
        picks.append(ik)

    sel2 = sel.reshape(N_EXPERTS, tm)
    prefix = jnp.dot(sel2.astype(BF16), tri_ref[...], preferred_element_type=F32)
    rank3 = (run_ref[...] + prefix).reshape(G, P, tm)
    run_ref[...] = run_ref[...] + jnp.sum(sel2, axis=1, keepdims=True)
    cnt_ref[...] = run_ref[...].astype(I32)

    krow = lax.broadcasted_iota(I32, (TOP_K, tm), 0)
    idx_o = jnp.zeros((TOP_K, tm), I32)
    rank_o = jnp.zeros((TOP_K, tm), F32)
    gw_o = jnp.zeros((TOP_K, tm), F32)
    for kk, ik in enumerate(picks):
        hit = eidx == ik
        rk = sum01(jnp.where(hit, rank3, 0.0)).reshape(1, tm)
        sk = sum01(jnp.where(hit, sc3, 0.0)).reshape(1, tm)
        idx_o = jnp.where(krow == kk, ik.reshape(1, tm), idx_o)
        rank_o = jnp.where(krow == kk, rk, rank_o)
        gw_o = jnp.where(krow == kk, sk, gw_o)
    gw_o = gw_o / jnp.sum(gw_o, axis=0, keepdims=True) * ROUTED_SCALE
    idx_ref[...] = idx_o
    rank_ref[...] = rank_o.astype(I32)
    gw_ref[...] = gw_o


def router(x, w_router, r_bias):
    n, d = x.shape
    tm = min(ROUTER_TILE, n)
    tri = jnp.triu(jnp.ones((tm, tm), BF16), k=1)
    kspec = pl.BlockSpec((TOP_K, tm), lambda i: (0, i))
    return pl.pallas_call(
        _router_kernel,
        grid=(n // tm,),
        in_specs=[pl.BlockSpec((N_EXPERTS, d), lambda i: (0, 0)),
                  pl.BlockSpec((N_EXPERTS, 1), lambda i: (0, 0)),
                  pl.BlockSpec((tm, tm), lambda i: (0, 0)),
                  pl.BlockSpec((tm, d), lambda i: (i, 0))],
        out_specs=[kspec, kspec, kspec, pl.BlockSpec((N_EXPERTS, 1), lambda i: (0, 0))],
        out_shape=[jax.ShapeDtypeStruct((TOP_K, n), I32), jax.ShapeDtypeStruct((TOP_K, n), I32),
                   jax.ShapeDtypeStruct((TOP_K, n), F32), jax.ShapeDtypeStruct((N_EXPERTS, 1), I32)],
        scratch_shapes=[pltpu.VMEM((N_EXPERTS, 1), F32)],
        compiler_params=_params("arbitrary"),
        name="router",
    )(w_router.T, r_bias.astype(F32).reshape(N_EXPERTS, 1), tri, x)


def _dispatch_kernel(dest_ref, pend_ref, pcnt_ref, xp_ref, xg_ref, zero_ref, sem, *, n_tok):
    i = pl.program_id(0)
    tm = xp_ref.shape[0] * SUBLANE

    def zero_block(start):
        return pltpu.make_async_copy(zero_ref, xg_ref.at[pl.ds(pl.multiple_of(start, MOE_BLOCK), MOE_BLOCK)], sem)

    @pl.when(i == 0)
    def _():
        zero_ref[...] = jnp.zeros_like(zero_ref)
        n_blocks = xg_ref.shape[0] // MOE_BLOCK
        n_used = pend_ref[N_EXPERTS - 1] // MOE_BLOCK

        def start(e, c):
            @pl.when(pcnt_ref[e] > 0)
            def _():
                zero_block(pend_ref[e] - MOE_BLOCK).start()
            return c

        def wait(e, c):
            @pl.when(pcnt_ref[e] > 0)
            def _():
                zero_block(pend_ref[e] - MOE_BLOCK).wait()
            return c

        def start_tail(blk, c):
            zero_block(blk * MOE_BLOCK).start()
            return c

        def wait_tail(blk, c):
            zero_block(blk * MOE_BLOCK).wait()
            return c

        lax.fori_loop(0, N_EXPERTS, start, 0)
        lax.fori_loop(n_used, n_blocks, start_tail, 0)
        lax.fori_loop(0, N_EXPERTS, wait, 0)
        lax.fori_loop(n_used, n_blocks, wait_tail, 0)

    def issue(grp, c):
        for s in range(SUBLANE):
            for k in range(TOP_K):
                d = dest_ref[k * n_tok + i * tm + grp * SUBLANE + s]
                pltpu.make_async_copy(xp_ref.at[grp, pl.ds(s, 1)], xg_ref.at[pl.ds(d, 1)],
                                      sem).start(priority=k % 2)
        return c

    lax.fori_loop(0, tm // SUBLANE, issue, 0)
    for _ in range(TOP_K):
        pltpu.make_async_copy(zero_ref.at[pl.ds(0, tm)], xg_ref.at[pl.ds(0, tm)], sem).wait()


def dispatch(xp, dest, pends, pcounts, *, n_rows):
    n, half = xp.shape
    tm = min(DISPATCH_TILE, n)
    assert tm <= MOE_BLOCK
    grid_spec = pltpu.PrefetchScalarGridSpec(
        num_scalar_prefetch=3,
        grid=(n // tm,),
        in_specs=[pl.BlockSpec((tm // SUBLANE, SUBLANE, half), lambda i, *_: (i, 0, 0))],
        out_specs=pl.BlockSpec(memory_space=pl.ANY),
        scratch_shapes=[pltpu.VMEM((MOE_BLOCK, half), U32), pltpu.SemaphoreType.DMA],
    )
    return pl.pallas_call(
        functools.partial(_dispatch_kernel, n_tok=n),
        grid_spec=grid_spec,
        out_shape=jax.ShapeDtypeStruct((n_rows, half), U32),
        compiler_params=_params("arbitrary"),
        name="moe_dispatch",
    )(dest, pends, pcounts, xp.reshape(n // SUBLANE, SUBLANE, half))


def _expert_kernel(be_ref, nx_ref, nu_ref, x_ref, wg_hbm, wu_hbm, wd_hbm, y_ref,
                   wg_st, wu_st, wd_st, wgu_s, wd_s, sem, *, layer):
    i = pl.program_id(0)
    used = i < nu_ref[0]
    expert = be_ref[i]
    first = jnp.logical_or(i == 0, expert != be_ref[jnp.maximum(i - 1, 0)])

    def fetch(e):
        return (pltpu.make_async_copy(wg_hbm.at[layer, e], wg_st, sem.at[0]),
                pltpu.make_async_copy(wu_hbm.at[layer, e], wu_st, sem.at[1]),
                pltpu.make_async_copy(wd_hbm.at[layer, e], wd_st, sem.at[2]))

    @pl.when(i == 0)
    def _():
        for c in fetch(expert):
            c.start()

    @pl.when(jnp.logical_and(used, first))
    def _():
        for c in fetch(expert):
            c.wait()
        wgu_s[:, :D_EXPERT] = wg_st[...].astype(BF16)
        wgu_s[:, D_EXPERT:] = wu_st[...].astype(BF16)
        wd_s[...] = wd_st[...].astype(BF16)
        nxt = nx_ref[i]

        @pl.when(nxt >= 0)
        def _():
            for c in fetch(nxt):
                c.start()

    @pl.when(used)
    def _():
        sub = MOE_BLOCK // EXPERT_SPLIT
        hs = []
        for r in range(EXPERT_SPLIT):
            lo, hi = _unpack_rows(x_ref[r * sub:(r + 1) * sub, :])
            x = jnp.concatenate([lo, hi], axis=1).astype(BF16)
            hs.append(jnp.dot(x, wgu_s[...], preferred_element_type=F32))
        for r, h in enumerate(hs):
            hid = (_silu(h[:, :D_EXPERT]) * h[:, D_EXPERT:]).astype(BF16)
            y_ref[r * sub:(r + 1) * sub, :] = _pack_rows(jnp.dot(hid, wd_s[...], preferred_element_type=F32))

    @pl.when(jnp.logical_not(used))
    def _():
        y_ref[...] = jnp.zeros_like(y_ref)


def expert_blocks(xg, blk_e, blk_next, n_used, w_gate, w_up, w_down, layer):
    n_rows, half = xg.shape
    d = 2 * half
    n_blocks = n_rows // MOE_BLOCK

    def xmap(i, be, nx, nu):
        return (jnp.minimum(i, nu[0] - 1), 0)

    hbm = pl.BlockSpec(memory_space=pl.ANY)
    grid_spec = pltpu.PrefetchScalarGridSpec(
        num_scalar_prefetch=3,
        grid=(n_blocks,),
        in_specs=[pl.BlockSpec((MOE_BLOCK, half), xmap), hbm, hbm, hbm],
        out_specs=pl.BlockSpec((MOE_BLOCK, half), lambda i, be, nx, nu: (i, 0)),
        scratch_shapes=[pltpu.VMEM((d, D_EXPERT), F32), pltpu.VMEM((d, D_EXPERT), F32),
                        pltpu.VMEM((D_EXPERT, d), F32),
                        pltpu.VMEM((d, 2 * D_EXPERT), BF16), pltpu.VMEM((D_EXPERT, d), BF16),
                        pltpu.SemaphoreType.DMA((3,))],
    )
    return pl.pallas_call(
        functools.partial(_expert_kernel, layer=layer),
        grid_spec=grid_spec,
        out_shape=jax.ShapeDtypeStruct((n_rows, half), U32),
        compiler_params=_params("arbitrary"),
        name="expert_blocks",
    )(blk_e, blk_next, n_used, xg, w_gate, w_up, w_down)


def _dest_kernel(ps_ref, idx_ref, rank_ref, dest_ref):
    idx = idx_ref[...]
    dest = rank_ref[...]
    for e in range(N_EXPERTS):
        dest = dest + jnp.where(idx == e, ps_ref[e], 0)
    dest_ref[...] = dest


def dispatch_rows(idx, rank, pstarts, *, tn=2048):
    k, n = idx.shape
    tn = min(tn, n)
    spec = pl.BlockSpec((k, tn), lambda i, ps: (0, i))
    return pl.pallas_call(
        _dest_kernel,
        grid_spec=pltpu.PrefetchScalarGridSpec(num_scalar_prefetch=1, grid=(n // tn,),
                                               in_specs=[spec, spec], out_specs=spec),
        out_shape=jax.ShapeDtypeStruct((k, n), I32),
        compiler_params=_params("arbitrary"),
        name="dispatch_rows",
    )(pstarts, idx, rank)


def _combine_kernel(dest_ref, x_ref, xb_ref, gw_ref, sg_ref, su_ref, sd_ref, g_ref, b_ref, yg_ref,
                    y_ref, yb_ref, yp_ref, buf_a, buf_b, sem, *, n_tok):
    i = pl.program_id(0)
    n_steps = pl.num_programs(0)
    tm = x_ref.shape[0] // 2
    half = x_ref.shape[1] // 2
    bufs = (buf_a, buf_b)

    def row_copy(tile, which, grp, s, k):
        d = dest_ref[k * n_tok + tile * tm + grp * SUBLANE + s]
        return pltpu.make_async_copy(yg_ref.at[pl.ds(d, 1)], bufs[which].at[k, grp, pl.ds(s, 1)], sem.at[which])

    def wait_tile(which):
        for k in range(TOP_K):
            pltpu.make_async_copy(yg_ref.at[pl.ds(0, tm)], yp_ref.at[pl.ds(0, tm)], sem.at[which]).wait()

    @pl.when(i == 0)
    def _():
        def issue(grp, c):
            for s in range(SUBLANE):
                for k in range(TOP_K):
                    row_copy(0, 0, grp, s, k).start(priority=k % 2)
            return c
        lax.fori_loop(0, tm // SUBLANE, issue, 0)

    for which in range(2):
        rows = slice(which * tm, (which + 1) * tm)
        wait_tile(which)
        nxt = jnp.minimum(2 * i + which + 1, 2 * n_steps - 1)
        for grp in range(tm // SUBLANE):
            for s in range(SUBLANE):
                for k in range(TOP_K):
                    row_copy(nxt, 1 - which, grp, s, k).start(priority=k % 2)

        xb = xb_ref[rows, :]
        hg = jnp.dot(xb, sg_ref[...], preferred_element_type=F32)
        hu = jnp.dot(xb, su_ref[...], preferred_element_type=F32)
        hid = (_silu(hg) * hu).astype(BF16)
        f = jnp.dot(hid, sd_ref[...], preferred_element_type=F32)

        gw = gw_ref[rows, :]
        lo = jnp.zeros((tm, half), F32)
        hi = jnp.zeros((tm, half), F32)
        for k in range(TOP_K):
            l_k, h_k = _unpack_rows(bufs[which][k].reshape(tm, half))
            lo = lo + gw[:, k:k + 1] * l_k
            hi = hi + gw[:, k:k + 1] * h_k
        f = f + jnp.concatenate([lo, hi], axis=1)
        y = _layer_norm_rows(DEEPNORM_ALPHA * x_ref[rows, :] + f, g_ref[...], b_ref[...])
        y_ref[rows, :] = y
        yb_ref[rows, :] = y.astype(BF16)
        yp_ref[rows, :] = _pack_rows(y)

    @pl.when(i == n_steps - 1)
    def _():
        wait_tile(0)


def moe_combine(x, xb, gw, dest, yg, s_gate, s_up, s_down, g, b):
    n, d = x.shape
    tm = min(COMBINE_TILE, n // 2)
    once = pl.Buffered(1)
    row = pl.BlockSpec((2 * tm, d), lambda i, *_: (i, 0))
    prow = pl.BlockSpec((2 * tm, d // 2), lambda i, *_: (i, 0))
    vec = pl.BlockSpec((1, d), lambda i, *_: (0, 0), pipeline_mode=once)
    gather_buf = pltpu.VMEM((TOP_K, tm // SUBLANE, SUBLANE, d // 2), U32)
    grid_spec = pltpu.PrefetchScalarGridSpec(
        num_scalar_prefetch=1,
        grid=(n // (2 * tm),),
        in_specs=[row, row, pl.BlockSpec((2 * tm, TOP_K), lambda i, *_: (i, 0)),
                  pl.BlockSpec((d, D_EXPERT), lambda i, *_: (0, 0), pipeline_mode=once),
                  pl.BlockSpec((d, D_EXPERT), lambda i, *_: (0, 0), pipeline_mode=once),
                  pl.BlockSpec((D_EXPERT, d), lambda i, *_: (0, 0), pipeline_mode=once),
                  vec, vec, pl.BlockSpec(memory_space=pl.ANY)],
        out_specs=[row, row, prow],
        scratch_shapes=[gather_buf, gather_buf, pltpu.SemaphoreType.DMA((2,))],
    )
    return pl.pallas_call(
        functools.partial(_combine_kernel, n_tok=n),
        grid_spec=grid_spec,
        out_shape=[jax.ShapeDtypeStruct((n, d), F32), jax.ShapeDtypeStruct((n, d), BF16),
                   jax.ShapeDtypeStruct((n, d // 2), U32)],
        compiler_params=_params("arbitrary"),
        name="moe_combine",
    )(dest, x, xb, gw, s_gate, s_up, s_down, g.reshape(1, d), b.reshape(1, d), yg)


def moe_ffn(x, xb, xp, layer, w_router, r_bias, w_gate, w_up, w_down, s_gate, s_up, s_down, ln_g, ln_b):
    n_tok = x.shape[0]
    idx, rank, gw, counts = router(x, w_router, r_bias)
    counts = counts.reshape(N_EXPERTS)
    pcounts = (counts + MOE_BLOCK - 1) // MOE_BLOCK * MOE_BLOCK
    pends = jnp.cumsum(pcounts).astype(I32)
    pstarts = pends - pcounts
    n_blocks = n_tok * TOP_K // MOE_BLOCK + N_EXPERTS
    blk_start = jnp.arange(n_blocks, dtype=I32) * MOE_BLOCK
    blk_e = jnp.minimum(jnp.sum(pends[None, :] <= blk_start[:, None], axis=1), N_EXPERTS - 1).astype(I32)
    n_used = (pends[-1:] // MOE_BLOCK).astype(I32)
    eids = jnp.arange(N_EXPERTS, dtype=I32)
    later = jnp.where((eids[None, :] > eids[:, None]) & (pcounts[None, :] > 0), eids[None, :], N_EXPERTS)
    nxt_e = jnp.min(later, axis=1)
    nxt_e = jnp.where(nxt_e == N_EXPERTS, -1, nxt_e)
    blk_next = jnp.sum(jnp.where(blk_e[:, None] == eids[None, :], nxt_e[None, :], 0), axis=1).astype(I32)
    dest = dispatch_rows(idx, rank, pstarts.astype(I32)).reshape(TOP_K * n_tok)

    xg = dispatch(xp, dest, pends, pcounts.astype(I32), n_rows=n_blocks * MOE_BLOCK)
    yg = expert_blocks(xg, blk_e, blk_next, n_used, w_gate, w_up, w_down, layer)
    return moe_combine(x, xb, gw.T, dest, yg, s_gate, s_up, s_down, ln_g, ln_b)


def kernel(x, ln_g, ln_b, ev_w_in, ev_decay_exp, ev_gn_w, ev_rpb, ev_w_out, od_w_in, od_lb, od_norm_w, od_w_out, moe_w_router, moe_bias, moe_w_gate, moe_w_up, moe_w_down, sh_w_gate, sh_w_up, sh_w_down):
    batch, seq, d = x.shape
    n = batch * seq
    xf = x.reshape(n, d).astype(F32)
    xb = xf.astype(BF16)
    for layer in range(DEPTH):
        j = layer // 2
        if layer % 2 == 0:
            ph = matmul(xb, ev_w_in, j, out_dtype=BF16, heads=True)
            o_ret = retention(ph, ev_decay_exp[j], ev_gn_w[j].astype(F32), batch=batch, seq=seq)
            o_na = neighborhood_attention(ph, ev_rpb[j], batch=batch, seq=seq, slab0=4 * RET_W // LANE)
            mixed = jnp.concatenate([o_ret, o_na], axis=1)
            w_out = ev_w_out
        else:
            ph = matmul(xb, od_w_in, j, out_dtype=BF16, heads=True)
            mixed = hgrn2(ph, od_lb, od_norm_w[j].astype(F32), layer=layer, batch=batch, seq=seq)
            w_out = od_w_out
        m = matmul(mixed, w_out, j, out_dtype=F32, heads=False)
        xf, xb, xp = ln_residual(xf, m, ln_g[layer, 0].astype(F32), ln_b[layer, 0].astype(F32))
        xf, xb, xp = moe_ffn(xf, xb, xp, layer, moe_w_router[layer].astype(F32), moe_bias[layer],
                             moe_w_gate, moe_w_up, moe_w_down, sh_w_gate[layer].astype(BF16),
                             sh_w_up[layer].astype(BF16), sh_w_down[layer].astype(BF16),
                             ln_g[layer, 1].astype(F32), ln_b[layer, 1].astype(F32))
    return xf.reshape(batch, seq, d)
```
